```python
import jax, jax.numpy as jnp
from jax import lax
import numpy as np

D_MODEL = 1024
BATCH = 8
SEQ = 4096
DEPTH = 1

ATT_HEADS = 8
ATT_HEAD_DIM = 128
ATT_WIDTH = ATT_HEADS * ATT_HEAD_DIM
MOBA_BLOCK = 256
MOBA_TOPK = 3
Q_CHUNK = 128
ROPE_THETA = 500000.0
ROPE_DIM = ATT_HEAD_DIM // 4

GDN_HEADS = 8
GDN_KDIM = 128
GDN_VDIM = 128
GDN_KWIDTH = GDN_HEADS * GDN_KDIM
GDN_VWIDTH = GDN_HEADS * GDN_VDIM
CONV_WIDTH = 4
GDN_CHUNK = 64

EPS = 1e-6
NEG = -1e30

IN_SPLITS = [ATT_WIDTH, ATT_WIDTH, ATT_WIDTH, ATT_WIDTH,
             GDN_KWIDTH, GDN_KWIDTH, GDN_VWIDTH, GDN_VWIDTH,
             GDN_HEADS, GDN_HEADS,
             D_MODEL, D_MODEL]
IN_COLS = int(sum(IN_SPLITS))
IN_OFFSETS = [int(o) for o in np.cumsum(IN_SPLITS)[:-1]]

kernel_name = "moba_gdn_gated_hybrid_block"


def rmsnorm(x, w):
    xf = x.astype(jnp.float32)
    y = xf * lax.rsqrt(jnp.mean(xf * xf, axis=-1, keepdims=True) + EPS)
    return (y * w.astype(jnp.float32)).astype(x.dtype)


def l2norm(x):
    return x * lax.rsqrt(jnp.sum(x * x, axis=-1, keepdims=True) + EPS)


def partial_rope(x, pos):
    half = ROPE_DIM // 2
    inv_freq = jnp.power(ROPE_THETA, -jnp.arange(half, dtype=jnp.float32) * (2.0 / ROPE_DIM))
    ang = pos.astype(jnp.float32)[:, None] * inv_freq[None, :]
    cos, sin = jnp.cos(ang), jnp.sin(ang)
    xr = x[..., :ROPE_DIM].astype(jnp.float32)
    x1, x2 = xr[..., :half], xr[..., half:]
    rot = jnp.concatenate([x1 * cos - x2 * sin, x2 * cos + x1 * sin], axis=-1).astype(x.dtype)
    return jnp.concatenate([rot, x[..., ROPE_DIM:]], axis=-1)


def moba_attention(q, k, v):
    bsz, nh, s, dh = q.shape
    nb = -(-s // MOBA_BLOCK)
    pad = nb * MOBA_BLOCK - s
    nc = s // Q_CHUNK
    kk = min(MOBA_TOPK, nb)
    scale = dh ** -0.5
    padw = ((0, 0), (0, 0), (0, pad), (0, 0))
    kblk = jnp.pad(k, padw).reshape(bsz, nh, nb, MOBA_BLOCK, dh)
    vblk = jnp.pad(v, padw).reshape(bsz, nh, nb, MOBA_BLOCK, dh)
    kmean = jnp.mean(kblk.astype(jnp.float32), axis=3)
    qch = q.reshape(bsz, nh, nc, Q_CHUNK, dh)
    head_idx = jnp.arange(nh)[:, None, None]

    def per_batch(args):
        q_b, k_b, v_b, km_b = args

        def per_chunk(c):
            q_c = lax.dynamic_index_in_dim(q_b, c, axis=1, keepdims=False)
            q_pos = c * Q_CHUNK + jnp.arange(Q_CHUNK)
            blk = (c * Q_CHUNK) // MOBA_BLOCK
            gate = jnp.einsum('hqd,hnd->hqn', q_c.astype(jnp.float32), km_b)
            gate = jnp.where((jnp.arange(nb) < blk)[None, None, :], gate, -jnp.inf)
            _, idx = lax.top_k(gate, kk)
            sel_valid = jnp.arange(kk) < blk
            k_sel = k_b[head_idx, idx]
            v_sel = v_b[head_idx, idx]
            s_sel = jnp.einsum('hqd,hqjld->hqjl', q_c, k_sel).astype(jnp.float32) * scale
            s_sel = jnp.where(sel_valid[None, None, :, None], s_sel, NEG)
            k_own = lax.dynamic_index_in_dim(k_b, blk, axis=1, keepdims=False)
            v_own = lax.dynamic_index_in_dim(v_b, blk, axis=1, keepdims=False)
            s_own = jnp.einsum('hqd,hld->hql', q_c, k_own).astype(jnp.float32) * scale
            k_pos = blk * MOBA_BLOCK + jnp.arange(MOBA_BLOCK)
            s_own = jnp.where(k_pos[None, None, :] <= q_pos[None, :, None], s_own, NEG)
            scores = jnp.concatenate([s_sel.reshape(nh, Q_CHUNK, kk * MOBA_BLOCK), s_own], axis=-1)
            p = jax.nn.softmax(scores, axis=-1).astype(v_b.dtype)
            p_sel = p[..., :kk * MOBA_BLOCK].reshape(nh, Q_CHUNK, kk, MOBA_BLOCK)
            p_own = p[..., kk * MOBA_BLOCK:]
            return (jnp.einsum('hqjl,hqjld->hqd', p_sel, v_sel)
                    + jnp.einsum('hql,hld->hqd', p_own, v_own))

        return lax.map(per_chunk, jnp.arange(nc))

    out = lax.map(per_batch, (qch, kblk, vblk, kmean))
    return out.transpose(0, 1, 3, 2, 4).reshape(bsz, s, nh * dh)


def causal_dwconv(x, w):
    width = w.shape[0]
    return lax.conv_general_dilated(x, w[:, None, :], window_strides=(1,), padding=[(width - 1, 0)],
                                    dimension_numbers=('NWC', 'WIO', 'NWC'),
                                    feature_group_count=x.shape[-1])


def gated_deltanet(q, k, v, g, beta):
    bsz, s, nh, dk = q.shape
    dv = v.shape[-1]
    c = GDN_CHUNK
    n = s // c
    q = l2norm(q) * (dk ** -0.5)
    k = l2norm(k)

    def chunks(t):
        return t.reshape(bsz, n, c, nh, -1).transpose(0, 3, 1, 2, 4)

    qc, kc, vc = chunks(q), chunks(k), chunks(v)
    gc = g.reshape(bsz, n, c, nh).transpose(0, 3, 1, 2)
    bc = beta.reshape(bsz, n, c, nh).transpose(0, 3, 1, 2)
    G = jnp.cumsum(gc, axis=-1)
    causal = jnp.tril(jnp.ones((c, c), dtype=bool))
    strict = jnp.tril(jnp.ones((c, c), dtype=bool), k=-1)
    diff = G[..., :, None] - G[..., None, :]
    decay = jnp.where(causal, jnp.exp(jnp.where(causal, diff, 0.0)), 0.0)
    k_beta = kc * bc[..., None]
    m = jnp.where(strict, jnp.einsum('bhnid,bhnjd->bhnij', k_beta, kc) * decay, 0.0)
    a_mat = m + jnp.eye(c, dtype=m.dtype)
    rhs = jnp.concatenate([vc * bc[..., None], k_beta * jnp.exp(G)[..., None]], axis=-1)
    sol = lax.linalg.triangular_solve(a_mat, rhs, left_side=True, lower=True, unit_diagonal=True)
    u, w = sol[..., :dv], sol[..., dv:]
    qk = jnp.where(causal, jnp.einsum('bhnid,bhnjd->bhnij', qc, kc) * decay, 0.0)
    q_dec = qc * jnp.exp(G)[..., None]
    k_dec = kc * jnp.exp(G[..., -1:] - G)[..., None]
    g_last = jnp.exp(G[..., -1])

    def step(state, xs):
        u_i, w_i, q_i, qk_i, k_i, gl_i = xs
        v_new = u_i - jnp.einsum('bhcd,bhde->bhce', w_i, state)
        o_i = jnp.einsum('bhcd,bhde->bhce', q_i, state) + jnp.einsum('bhij,bhje->bhie', qk_i, v_new)
        state = state * gl_i[..., None, None] + jnp.einsum('bhcd,bhce->bhde', k_i, v_new)
        return state, o_i

    xs = tuple(jnp.moveaxis(t, 2, 0) for t in (u, w, q_dec, qk, k_dec, g_last))
    state0 = jnp.zeros((bsz, nh, dk, dv), dtype=jnp.float32)
    _, o = lax.scan(step, state0, xs)
    return o.transpose(1, 0, 3, 2, 4).reshape(bsz, s, nh, dv)


def hybrid_layer(x, pre_norm_w, w_in, conv_w, a_log, dt_bias, gdn_norm_w,
                 w_branch_a, w_branch_b, w_out, post_norm_w):
    bsz, s, _ = x.shape
    f32 = jnp.float32
    h = rmsnorm(x, pre_norm_w)
    proj = jnp.einsum('bsd,de->bse', h, w_in)
    (q_a, k_a, v_a, z_a, q_b, k_b, v_b, z_b,
     beta_logit, decay_logit, gate_a, gate_b) = jnp.split(proj, IN_OFFSETS, axis=-1)

    pos = jnp.arange(s)

    def heads(t):
        return t.reshape(bsz, s, ATT_HEADS, ATT_HEAD_DIM).transpose(0, 2, 1, 3)

    o_a = moba_attention(partial_rope(heads(q_a), pos), partial_rope(heads(k_a), pos), heads(v_a))
    y_a = jnp.einsum('bse,ed->bsd', o_a * jax.nn.silu(z_a), w_branch_a)

    qkv = jax.nn.silu(causal_dwconv(jnp.concatenate([q_b, k_b, v_b], axis=-1), conv_w))
    q_b, k_b, v_b = jnp.split(qkv, [GDN_KWIDTH, 2 * GDN_KWIDTH], axis=-1)
    g = -jnp.exp(a_log.astype(f32)) * jax.nn.softplus(decay_logit.astype(f32) + dt_bias.astype(f32))
    beta = jax.nn.sigmoid(beta_logit.astype(f32))
    o_b = gated_deltanet(q_b.reshape(bsz, s, GDN_HEADS, GDN_KDIM).astype(f32),
                         k_b.reshape(bsz, s, GDN_HEADS, GDN_KDIM).astype(f32),
                         v_b.reshape(bsz, s, GDN_HEADS, GDN_VDIM).astype(f32), g, beta)
    o_b = rmsnorm(o_b, gdn_norm_w).reshape(bsz, s, GDN_VWIDTH).astype(x.dtype)
    y_b = jnp.einsum('bse,ed->bsd', o_b * jax.nn.silu(z_b), w_branch_b)

    merged = jax.nn.sigmoid(gate_a) * y_a + jax.nn.sigmoid(gate_b) * y_b
    out = jnp.einsum('bsd,de->bse', merged, w_out)
    return x + rmsnorm(out, post_norm_w)


def setup_inputs(seed: int = 0) -> dict:
    key = jax.random.key(seed)
    ks = jax.random.split(key, 12)
    conv_ch = 2 * GDN_KWIDTH + GDN_VWIDTH
    x = jax.random.normal(ks[0], (BATCH, SEQ, D_MODEL), jnp.float32)
    pre_norm_w = 1.0 + 0.05 * jax.random.normal(ks[1], (DEPTH, D_MODEL), jnp.float32)
    w_in = jax.random.normal(ks[2], (DEPTH, D_MODEL, IN_COLS), jnp.float32) * D_MODEL ** -0.5
    conv_w = jax.random.normal(ks[3], (DEPTH, CONV_WIDTH, conv_ch), jnp.float32) * CONV_WIDTH ** -0.5
    a_log = jnp.log(jax.random.uniform(ks[4], (DEPTH, GDN_HEADS), jnp.float32, 1.0, 16.0))
    dt = jnp.exp(jax.random.uniform(ks[5], (DEPTH, GDN_HEADS), jnp.float32,
                                    float(np.log(1e-3)), float(np.log(1e-1))))
    dt_bias = dt + jnp.log(-jnp.expm1(-dt))
    gdn_norm_w = 1.0 + 0.05 * jax.random.normal(ks[6], (DEPTH, GDN_VDIM), jnp.float32)
    w_branch_a = jax.random.normal(ks[7], (DEPTH, ATT_WIDTH, D_MODEL), jnp.float32) * ATT_WIDTH ** -0.5
    w_branch_b = jax.random.normal(ks[8], (DEPTH, GDN_VWIDTH, D_MODEL), jnp.float32) * GDN_VWIDTH ** -0.5
    w_out = jax.random.normal(ks[9], (DEPTH, D_MODEL, D_MODEL), jnp.float32) * D_MODEL ** -0.5
    post_norm_w = 1.0 + 0.05 * jax.random.normal(ks[10], (DEPTH, D_MODEL), jnp.float32)
    return {"x": x, "pre_norm_w": pre_norm_w, "w_in": w_in, "conv_w": conv_w,
            "a_log": a_log, "dt_bias": dt_bias, "gdn_norm_w": gdn_norm_w,
            "w_branch_a": w_branch_a, "w_branch_b": w_branch_b, "w_out": w_out,
            "post_norm_w": post_norm_w}


def reference(x, pre_norm_w, w_in, conv_w, a_log, dt_bias, gdn_norm_w,
              w_branch_a, w_branch_b, w_out, post_norm_w):
    for layer in range(DEPTH):
        x = hybrid_layer(x, pre_norm_w[layer], w_in[layer], conv_w[layer], a_log[layer],
                         dt_bias[layer], gdn_norm_w[layer], w_branch_a[layer],
                         w_branch_b[layer], w_out[layer], post_norm_w[layer])
    return x
```

```python
import functools

import jax
import jax.numpy as jnp
import numpy as np
from jax import lax
from jax.experimental import pallas as pl
from jax.experimental.pallas import tpu as pltpu

F32 = jnp.float32
BF16 = jnp.bfloat16

D_MODEL = 1024
N_HEADS = 8
HEAD_DIM = 128
MOBA_BLOCK = 256
MOBA_TOPK = 3
ROPE_THETA = 500000.0
ROPE_DIM = HEAD_DIM // 4
CONV_WIDTH = 4
GDN_CHUNK = 64
EPS = 1e-6
NEG = -1e30
GATE_MASKED = -3.0e38
LANES = 128
N_MAIN_BLOCKS = 10

COL_QA, COL_KA, COL_VA, COL_ZA = 0, 8, 16, 24
COL_QB, COL_KB, COL_VB, COL_ZB = 32, 40, 48, 56
BLK_GATE_A, BLK_GATE_B = 8, 9

VMEM_LIMIT = 56 * 1024 * 1024
HIGHEST = lax.Precision.HIGHEST


def _dot(a, b, precision=None):
    return jnp.dot(a, b, preferred_element_type=F32, precision=precision)


def _dot_nt(a, b, precision=None):
    return lax.dot_general(a, b, (((1,), (1,)), ((), ())),
                           preferred_element_type=F32, precision=precision)


def _dot_tn(a, b, precision=None):
    return lax.dot_general(a, b, (((0,), (0,)), ((), ())),
                           preferred_element_type=F32, precision=precision)


def _silu(x):
    return x * (1.0 / (1.0 + jnp.exp(-x)))


def _sigmoid(x):
    return 1.0 / (1.0 + jnp.exp(-x))


def _proj_kernel(x_ref, nw_ref, w_ref, wsh_ref, wsl_ref, o_ref, os_ref, h_ref):
    j = pl.program_id(1)

    @pl.when(j == 0)
    def _():
        x = x_ref[...]
        ms = jnp.mean(x * x, axis=-1, keepdims=True)
        h = x * lax.rsqrt(ms + EPS) * nw_ref[...]
        h_hi = h.astype(BF16)
        h_lo = (h - h_hi.astype(F32)).astype(BF16)
        h_ref[...] = h_hi
        os_ref[...] = (_dot(h_hi, wsh_ref[...]) + _dot(h_lo, wsh_ref[...])
                       + _dot(h_hi, wsl_ref[...]))

    o_ref[...] = _dot(h_ref[...], w_ref[...]).astype(o_ref.dtype)


def _in_projection(x2, pre_norm_w, w_main, ws_hi, ws_lo, tm=1024):
    t = x2.shape[0]
    return pl.pallas_call(
        _proj_kernel,
        grid=(t // tm, N_MAIN_BLOCKS),
        in_specs=[
            pl.BlockSpec((tm, D_MODEL), lambda i, j: (i, 0)),
            pl.BlockSpec((1, D_MODEL), lambda i, j: (0, 0)),
            pl.BlockSpec((D_MODEL, D_MODEL), lambda i, j: (0, j)),
            pl.BlockSpec((D_MODEL, LANES), lambda i, j: (0, 0)),
            pl.BlockSpec((D_MODEL, LANES), lambda i, j: (0, 0)),
        ],
        out_specs=[
            pl.BlockSpec((tm, D_MODEL), lambda i, j: (i, j)),
            pl.BlockSpec((tm, LANES), lambda i, j: (i, 0)),
        ],
        out_shape=[
            jax.ShapeDtypeStruct((t, N_MAIN_BLOCKS * D_MODEL), BF16),
            jax.ShapeDtypeStruct((t, LANES), F32),
        ],
        scratch_shapes=[pltpu.VMEM((tm, D_MODEL), BF16)],
        compiler_params=pltpu.CompilerParams(
            dimension_semantics=("arbitrary", "arbitrary"),
            vmem_limit_bytes=VMEM_LIMIT),
        name="in_projection",
    )(x2, pre_norm_w, w_main, ws_hi, ws_lo)


def _rope(x, cos_t, sin_t):
    half = ROPE_DIM // 2
    lane = lax.broadcasted_iota(jnp.int32, x.shape, 1)
    partner = jnp.where(lane < half, pltpu.roll(x, LANES - half, 1), pltpu.roll(x, half, 1))
    return x * cos_t + partner * sin_t


def _moba_kernel(q_ref, k_ref, v_ref, z_ref, cosk_ref, sink_ref, cosq_ref, sinq_ref,
                 o_ref, ka_ref, km_ref):
    t = pl.program_id(2)
    s_len = k_ref.shape[0]
    n_blocks = s_len // MOBA_BLOCK
    scale = HEAD_DIM ** -0.5

    @pl.when(t == 0)
    def _():
        kr = _rope(k_ref[...].astype(F32), cosk_ref[...], sink_ref[...])
        ka_ref[:, :HEAD_DIM] = kr.astype(BF16)
        row = lax.broadcasted_iota(jnp.int32, (s_len, LANES), 0)
        lane = lax.broadcasted_iota(jnp.int32, (s_len, LANES), 1)
        blk_of_row = lax.shift_right_logical(row, MOBA_BLOCK.bit_length() - 1)
        ka_ref[:, HEAD_DIM:] = jnp.where(blk_of_row == lane, 1.0, 0.0).astype(BF16)
        km_ref[...] = jnp.zeros_like(km_ref)
        km_ref[:n_blocks, :] = jnp.mean(kr.reshape(n_blocks, MOBA_BLOCK, HEAD_DIM), axis=1)

    q = _rope(q_ref[...].astype(F32), cosq_ref[...], sinq_ref[...])
    tq = q.shape[0]

    gate = _dot_nt(q, km_ref[...], precision=HIGHEST)
    lane = lax.broadcasted_iota(jnp.int32, gate.shape, 1).astype(F32)
    t_f = t.astype(F32)
    g = jnp.where(lane < t_f, gate, GATE_MASKED)
    sel = jnp.where(lane == t_f, 1.0, 0.0)
    for _ in range(MOBA_TOPK):
        m = jnp.max(g, axis=1, keepdims=True)
        cand = jnp.where((g == m) & (g > 0.5 * GATE_MASKED), lane, float(LANES))
        idx = jnp.min(cand, axis=1, keepdims=True)
        pick = lane == idx
        sel = jnp.where(pick, 1.0, sel)
        g = jnp.where(pick, GATE_MASKED, g)
    bias = jnp.where(sel > 0.5, 0.0, NEG)
    q_aug = jnp.concatenate([q.astype(BF16), bias.astype(BF16)], axis=1)

    r0 = pl.multiple_of(t * MOBA_BLOCK, MOBA_BLOCK)
    s0 = _dot_nt(q_aug, ka_ref[pl.ds(r0, MOBA_BLOCK), :]) * scale
    rr = lax.broadcasted_iota(jnp.int32, s0.shape, 0)
    cc = lax.broadcasted_iota(jnp.int32, s0.shape, 1)
    s0 = jnp.where(cc <= rr, s0, NEG)
    m0 = jnp.max(s0, axis=1, keepdims=True)
    p0 = jnp.exp(s0 - m0)
    l0 = jnp.sum(p0, axis=1, keepdims=True)
    acc0 = _dot(p0.astype(BF16), v_ref[pl.ds(r0, MOBA_BLOCK), :])

    def body(n, carry):
        m_run, l_run, acc = carry
        rn = pl.multiple_of(n * MOBA_BLOCK, MOBA_BLOCK)
        s = _dot_nt(q_aug, ka_ref[pl.ds(rn, MOBA_BLOCK), :]) * scale
        m_new = jnp.maximum(m_run, jnp.max(s, axis=1, keepdims=True))
        alpha = jnp.exp(m_run - m_new)
        p = jnp.exp(s - m_new)
        l_new = alpha * l_run + jnp.sum(p, axis=1, keepdims=True)
        acc_new = alpha * acc + _dot(p.astype(BF16), v_ref[pl.ds(rn, MOBA_BLOCK), :])
        return m_new, l_new, acc_new

    _, l_fin, acc = lax.fori_loop(0, t, body, (m0, l0, acc0))
    o = acc / l_fin
    o_ref[...] = (o * _silu(z_ref[...].astype(F32))).astype(o_ref.dtype)
    del tq


def _moba(proj3, cos_t, sin_t):
    b, s, _ = proj3.shape
    nt = s // MOBA_BLOCK
    seq_spec = lambda col: pl.BlockSpec((None, s, HEAD_DIM), lambda bi, h, t: (bi, 0, col + h))
    tile_spec = lambda col: pl.BlockSpec((None, MOBA_BLOCK, HEAD_DIM), lambda bi, h, t: (bi, t, col + h))
    return pl.pallas_call(
        _moba_kernel,
        grid=(b, N_HEADS, nt),
        in_specs=[
            tile_spec(COL_QA), seq_spec(COL_KA), seq_spec(COL_VA), tile_spec(COL_ZA),
            pl.BlockSpec((s, LANES), lambda bi, h, t: (0, 0)),
            pl.BlockSpec((s, LANES), lambda bi, h, t: (0, 0)),
            pl.BlockSpec((MOBA_BLOCK, LANES), lambda bi, h, t: (t, 0)),
            pl.BlockSpec((MOBA_BLOCK, LANES), lambda bi, h, t: (t, 0)),
        ],
        out_specs=pl.BlockSpec((None, MOBA_BLOCK, HEAD_DIM), lambda bi, h, t: (bi, t, h)),
        out_shape=jax.ShapeDtypeStruct((b, s, N_HEADS * HEAD_DIM), BF16),
        scratch_shapes=[
            pltpu.VMEM((s, 2 * HEAD_DIM), BF16),
            pltpu.VMEM((LANES, HEAD_DIM), F32),
        ],
        compiler_params=pltpu.CompilerParams(
            dimension_semantics=("arbitrary", "arbitrary", "arbitrary"),
            vmem_limit_bytes=VMEM_LIMIT),
        name="moba_attention",
    )(proj3, proj3, proj3, proj3, cos_t, sin_t, cos_t, sin_t)


def _gdn_kernel(q_ref, k_ref, v_ref, z_ref, sm_ref, cwq_ref, cwk_ref, cwv_ref,
                alog_ref, dtb_ref, nw_ref, o_ref,
                pad_s, q_s, k_s, v_s, w_s, qk_s, g_s, b_s, gl_s, oo_s):
    h = pl.program_id(1)
    s_len = q_ref.shape[0]
    c = GDN_CHUNK
    n_chunks = s_len // c
    pad = 8

    def conv_silu(x_ref, cw_ref):
        pad_s[0:pad, :] = jnp.zeros((pad, HEAD_DIM), F32)
        pad_s[pad:, :] = x_ref[...].astype(F32)
        cw = cw_ref[...]
        y = pad_s[pad:, :] * cw[CONV_WIDTH - 1:CONV_WIDTH, :]
        for back in range(1, CONV_WIDTH):
            y = y + pad_s[pl.ds(pad - back, s_len), :] * cw[CONV_WIDTH - 1 - back:CONV_WIDTH - back, :]
        return _silu(y)

    def l2n(x):
        return x * lax.rsqrt(jnp.sum(x * x, axis=-1, keepdims=True) + EPS)

    q_s[...] = l2n(conv_silu(q_ref, cwq_ref)) * (HEAD_DIM ** -0.5)
    k_s[...] = l2n(conv_silu(k_ref, cwk_ref))
    v_s[...] = conv_silu(v_ref, cwv_ref)

    sm = sm_ref[...]
    rsel = lax.broadcasted_iota(jnp.int32, (LANES, LANES), 0)
    sel_b = jnp.where(rsel == h, 1.0, 0.0)
    sel_d = jnp.where(rsel == h + N_HEADS, 1.0, 0.0)
    beta_logit = _dot(sm, sel_b, precision=HIGHEST)
    decay_logit = _dot(sm, sel_d, precision=HIGHEST)
    a_neg = -jnp.exp(alog_ref[...])
    xs = decay_logit + dtb_ref[...]
    softplus = jnp.maximum(xs, 0.0) + jnp.log(1.0 + jnp.exp(-jnp.abs(xs)))
    g_s[...] = a_neg * softplus
    b_s[...] = _sigmoid(beta_logit)

    ri = lax.broadcasted_iota(jnp.int32, (c, c), 0)
    ci = lax.broadcasted_iota(jnp.int32, (c, c), 1)
    causal = ci <= ri
    strict = ci < ri
    tri_incl = jnp.where(causal, 1.0, 0.0)
    su = jnp.where(ri > ci, 1.0, 0.0)
    eye = jnp.where(ci == ri, 1.0, 0.0)
    level_masks = []
    size = 1
    while size < c:
        sh = size.bit_length() - 1
        same_big = lax.shift_right_logical(ri, sh + 1) == lax.shift_right_logical(ci, sh + 1)
        diff_small = lax.shift_right_logical(ri, sh) != lax.shift_right_logical(ci, sh)
        level_masks.append(jnp.where(same_big & diff_small & strict, 1.0, 0.0))
        size *= 2

    def prep(n, carry):
        r = pl.ds(pl.multiple_of(n * c, c), c)
        q = q_s[r, :]
        k = k_s[r, :]
        v = v_s[r, :]
        gb = g_s[r, :]
        bb = b_s[r, :]
        diff = _dot(tri_incl, gb[:, :c] * su, precision=HIGHEST)
        gcum = _dot(tri_incl, gb, precision=HIGHEST)
        decay = jnp.where(causal, jnp.exp(jnp.where(causal, diff, 0.0)), 0.0)
        glast = gcum[c - 1:c, :]
        e_g = jnp.exp(gcum)
        kb = k * bb
        m = jnp.where(strict, _dot_nt(kb, k, precision=HIGHEST) * decay, 0.0)
        x = eye - m * level_masks[0]
        for lm in level_masks[1:]:
            lo = m * lm
            x = x - _dot(_dot(x, lo, precision=HIGHEST), x, precision=HIGHEST)
        u = _dot(x, v * bb, precision=HIGHEST)
        w = _dot(x, kb * e_g, precision=HIGHEST)
        qk = jnp.where(causal, _dot_nt(q, k, precision=HIGHEST) * decay, 0.0)
        v_s[r, :] = u
        w_s[r, :] = w
        q_s[r, :] = q * e_g
        k_s[r, :] = k * jnp.exp(glast - gcum)
        qk_s[r, :] = qk
        gl_s[pl.ds(n, 1), :] = jnp.exp(glast)
        return carry

    lax.fori_loop(0, n_chunks, prep, 0)

    def scan(n, state):
        r = pl.ds(pl.multiple_of(n * c, c), c)
        u = v_s[r, :]
        w = w_s[r, :]
        qd = q_s[r, :]
        kd = k_s[r, :]
        qk = qk_s[r, :]
        gl = gl_s[pl.ds(n, 1), :]
        v_new = u - _dot(w, state, precision=HIGHEST)
        oo_s[r, :] = _dot(qd, state, precision=HIGHEST) + _dot(qk, v_new, precision=HIGHEST)
        return state * gl + _dot_tn(kd, v_new, precision=HIGHEST)

    lax.fori_loop(0, n_chunks, scan, jnp.zeros((HEAD_DIM, HEAD_DIM), F32))

    o = oo_s[...]
    o = o * lax.rsqrt(jnp.mean(o * o, axis=-1, keepdims=True) + EPS) * nw_ref[...]
    o_ref[...] = (o * _silu(z_ref[...].astype(F32))).astype(o_ref.dtype)


def _gdn(proj3, small3, conv_w, alog_b, dtb_b, gdn_norm_w):
    b, s, _ = proj3.shape
    c = GDN_CHUNK
    seq_spec = lambda col: pl.BlockSpec((None, s, HEAD_DIM), lambda bi, h: (bi, 0, col + h))
    cw_spec = lambda off: pl.BlockSpec((CONV_WIDTH, HEAD_DIM), lambda bi, h: (0, off + h))
    head_row = pl.BlockSpec((None, 1, LANES), lambda bi, h: (h, 0, 0))
    seq_f32 = pltpu.VMEM((s, HEAD_DIM), F32)
    return pl.pallas_call(
        _gdn_kernel,
        grid=(b, N_HEADS),
        in_specs=[
            seq_spec(COL_QB), seq_spec(COL_KB), seq_spec(COL_VB), seq_spec(COL_ZB),
            pl.BlockSpec((None, s, LANES), lambda bi, h: (bi, 0, 0)),
            cw_spec(0), cw_spec(N_HEADS), cw_spec(2 * N_HEADS),
            head_row, head_row,
            pl.BlockSpec((1, HEAD_DIM), lambda bi, h: (0, 0)),
        ],
        out_specs=pl.BlockSpec((None, s, HEAD_DIM), lambda bi, h: (bi, 0, h)),
        out_shape=jax.ShapeDtypeStruct((b, s, N_HEADS * HEAD_DIM), BF16),
        scratch_shapes=[
            pltpu.VMEM((s + 8, HEAD_DIM), F32),
            seq_f32, seq_f32, seq_f32, seq_f32,
            pltpu.VMEM((s, c), F32),
            seq_f32, seq_f32,
            pltpu.VMEM((s // c, LANES), F32),
            seq_f32,
        ],
        compiler_params=pltpu.CompilerParams(
            dimension_semantics=("arbitrary", "arbitrary"),
            vmem_limit_bytes=VMEM_LIMIT),
        name="gated_deltanet",
    )(proj3, proj3, proj3, proj3, small3, conv_w, conv_w, conv_w, alog_b, dtb_b, gdn_norm_w)


def _out_kernel(x_ref, a_ref, b_ref, ga_ref, gb_ref, wa_ref, wb_ref, wo_ref, nw_ref, o_ref):
    y_a = _dot(a_ref[...], wa_ref[...])
    y_b = _dot(b_ref[...], wb_ref[...])
    merged = (_sigmoid(ga_ref[...].astype(F32)) * y_a
              + _sigmoid(gb_ref[...].astype(F32)) * y_b)
    out = _dot(merged.astype(BF16), wo_ref[...])
    ms = jnp.mean(out * out, axis=-1, keepdims=True)
    o_ref[...] = x_ref[...] + out * lax.rsqrt(ms + EPS) * nw_ref[...]


def _out_projection(x2, a2, b2, proj2, wa, wb, wo, post_norm_w, tm=512):
    t = x2.shape[0]
    tok = lambda: pl.BlockSpec((tm, D_MODEL), lambda i: (i, 0))
    full = lambda: pl.BlockSpec((D_MODEL, D_MODEL), lambda i: (0, 0))
    return pl.pallas_call(
        _out_kernel,
        grid=(t // tm,),
        in_specs=[
            tok(), tok(), tok(),
            pl.BlockSpec((tm, D_MODEL), lambda i: (i, BLK_GATE_A)),
            pl.BlockSpec((tm, D_MODEL), lambda i: (i, BLK_GATE_B)),
            full(), full(), full(),
            pl.BlockSpec((1, D_MODEL), lambda i: (0, 0)),
        ],
        out_specs=tok(),
        out_shape=jax.ShapeDtypeStruct((t, D_MODEL), F32),
        compiler_params=pltpu.CompilerParams(
            dimension_semantics=("arbitrary",),
            vmem_limit_bytes=VMEM_LIMIT),
        name="out_projection",
    )(x2, a2, b2, proj2, proj2, wa, wb, wo, post_norm_w)


def _rope_tables(s):
    half = ROPE_DIM // 2
    inv_freq = jnp.power(ROPE_THETA, -jnp.arange(half, dtype=F32) * (2.0 / ROPE_DIM))
    ang = jnp.arange(s, dtype=F32)[:, None] * inv_freq[None, :]
    cos, sin = jnp.cos(ang), jnp.sin(ang)
    ones = jnp.ones((s, HEAD_DIM - ROPE_DIM), F32)
    zeros = jnp.zeros((s, HEAD_DIM - ROPE_DIM), F32)
    cos_t = jnp.concatenate([cos, cos, ones], axis=1)
    sin_t = jnp.concatenate([-sin, sin, zeros], axis=1)
    return cos_t, sin_t


def _layer(x, pre_norm_w, w_in, conv_w, a_log, dt_bias, gdn_norm_w,
           w_branch_a, w_branch_b, w_out, post_norm_w):
    b, s, d = x.shape
    x2 = x.reshape(b * s, d)
    n_main = 8 * D_MODEL
    w_main = jnp.concatenate([w_in[:, :n_main], w_in[:, n_main + 2 * N_HEADS:]], axis=1).astype(BF16)
    w_small = jnp.pad(w_in[:, n_main:n_main + 2 * N_HEADS], ((0, 0), (0, LANES - 2 * N_HEADS)))
    ws_hi = w_small.astype(BF16)
    ws_lo = (w_small - ws_hi.astype(F32)).astype(BF16)

    proj, small = _in_projection(x2, pre_norm_w.reshape(1, d), w_main, ws_hi, ws_lo)
    proj3 = proj.reshape(b, s, N_MAIN_BLOCKS * D_MODEL)
    small3 = small.reshape(b, s, LANES)

    cos_t, sin_t = _rope_tables(s)
    act_a = _moba(proj3, cos_t, sin_t)

    alog_b = jnp.broadcast_to(a_log.astype(F32)[:, None, None], (N_HEADS, 1, LANES))
    dtb_b = jnp.broadcast_to(dt_bias.astype(F32)[:, None, None], (N_HEADS, 1, LANES))
    act_b = _gdn(proj3, small3, conv_w, alog_b, dtb_b, gdn_norm_w.reshape(1, HEAD_DIM))

    out = _out_projection(x2, act_a.reshape(b * s, d), act_b.reshape(b * s, d), proj,
                          w_branch_a.astype(BF16), w_branch_b.astype(BF16), w_out.astype(BF16),
                          post_norm_w.reshape(1, d))
    return out.reshape(b, s, d)


def kernel(x, pre_norm_w, w_in, conv_w, a_log, dt_bias, gdn_norm_w,
           w_branch_a, w_branch_b, w_out, post_norm_w):
    for layer in range(pre_norm_w.shape[0]):
        x = _layer(x, pre_norm_w[layer], w_in[layer], conv_w[layer], a_log[layer],
                   dt_bias[layer], gdn_norm_w[layer], w_branch_a[layer],
                   w_branch_b[layer], w_out[layer], post_norm_w[layer])
    return x
```

```python
import jax
import jax.numpy as jnp
from jax import lax
from jax.experimental import pallas as pl
from jax.experimental.pallas import tpu as pltpu

F32 = jnp.float32
BF16 = jnp.bfloat16

D_MODEL = 1024
N_HEADS = 8
HEAD_DIM = 128
MOBA_BLOCK = 256
MOBA_TOPK = 3
ROPE_THETA = 500000.0
ROPE_DIM = HEAD_DIM // 4
CONV_WIDTH = 4
GDN_CHUNK = 64
EPS = 1e-6
NEG = -1e30
GATE_MASKED = -3.0e38
LANES = 128
N_MAIN_BLOCKS = 10

COL_QA, COL_KA, COL_VA, COL_ZA = 0, 8, 16, 24
COL_QB, COL_KB, COL_VB, COL_ZB = 32, 40, 48, 56
BLK_GATE_A, BLK_GATE_B = 8, 9
LANE_BETA, LANE_DECAY = 0, N_HEADS

VMEM_LIMIT = 56 * 1024 * 1024
HIGHEST = lax.Precision.HIGHEST


def _dot(a, b, precision=None):
    return jnp.dot(a, b, preferred_element_type=F32, precision=precision)


def _dot_nt(a, b, precision=None):
    return lax.dot_general(a, b, (((1,), (1,)), ((), ())),
                           preferred_element_type=F32, precision=precision)


def _dot_tn(a, b, precision=None):
    return lax.dot_general(a, b, (((0,), (0,)), ((), ())),
                           preferred_element_type=F32, precision=precision)


def _silu(x):
    return x * (1.0 / (1.0 + jnp.exp(-x)))


def _sigmoid(x):
    return 1.0 / (1.0 + jnp.exp(-x))


def _shr(x, n):
    return lax.shift_right_logical(x, n)


def _proj_kernel(x_ref, nw_ref, w_ref, wsh_ref, wsl_ref, o_ref, os_ref, h_ref):
    j = pl.program_id(1)

    @pl.when(j == 0)
    def _():
        x = x_ref[...]
        ms = jnp.mean(x * x, axis=-1, keepdims=True)
        h = x * lax.rsqrt(ms + EPS) * nw_ref[...]
        h_hi = h.astype(BF16)
        h_lo = (h - h_hi.astype(F32)).astype(BF16)
        h_ref[...] = h_hi
        os_ref[...] = (_dot(h_hi, wsh_ref[...]) + _dot(h_lo, wsh_ref[...])
                       + _dot(h_hi, wsl_ref[...]))

    o_ref[...] = _dot(h_ref[...], w_ref[...]).astype(o_ref.dtype)


def _in_projection(x2, pre_norm_w, w_main, ws_hi, ws_lo, tm=2048):
    t = x2.shape[0]
    return pl.pallas_call(
        _proj_kernel,
        grid=(t // tm, N_MAIN_BLOCKS),
        in_specs=[
            pl.BlockSpec((tm, D_MODEL), lambda i, j: (i, 0)),
            pl.BlockSpec((1, D_MODEL), lambda i, j: (0, 0)),
            pl.BlockSpec((D_MODEL, D_MODEL), lambda i, j: (0, j)),
            pl.BlockSpec((D_MODEL, LANES), lambda i, j: (0, 0)),
            pl.BlockSpec((D_MODEL, LANES), lambda i, j: (0, 0)),
        ],
        out_specs=[
            pl.BlockSpec((tm, D_MODEL), lambda i, j: (i, j)),
            pl.BlockSpec((tm, LANES), lambda i, j: (i, 0)),
        ],
        out_shape=[
            jax.ShapeDtypeStruct((t, N_MAIN_BLOCKS * D_MODEL), BF16),
            jax.ShapeDtypeStruct((t, LANES), F32),
        ],
        scratch_shapes=[pltpu.VMEM((tm, D_MODEL), BF16)],
        compiler_params=pltpu.CompilerParams(
            dimension_semantics=("arbitrary", "arbitrary"),
            vmem_limit_bytes=VMEM_LIMIT),
        name="in_projection",
    )(x2, pre_norm_w, w_main, ws_hi, ws_lo)


def _rope(x, cos_t, sin_t):
    half = ROPE_DIM // 2
    lane = lax.broadcasted_iota(jnp.int32, x.shape, 1)
    partner = jnp.where(lane < half, pltpu.roll(x, LANES - half, 1), pltpu.roll(x, half, 1))
    return x * cos_t + partner * sin_t


def _split_bf16(x):
    hi = x.astype(BF16)
    return hi, (x - hi.astype(F32)).astype(BF16)


def _moba_kernel(q_ref, k_ref, v_ref, z_ref, cos_ref, sin_ref, o_ref, qa_s, ka_s, va_s):
    s_len = k_ref.shape[0]
    n_blocks = s_len // MOBA_BLOCK
    blk_shift = MOBA_BLOCK.bit_length() - 1
    scale = HEAD_DIM ** -0.5 * 1.4426950408889634
    cos_t = cos_ref[...]
    sin_t = sin_ref[...]
    row = lax.broadcasted_iota(jnp.int32, (s_len, LANES), 0)
    lane = lax.broadcasted_iota(jnp.int32, (s_len, LANES), 1)
    blk_of_row = _shr(row, blk_shift)

    kr = _rope(k_ref[...].astype(F32), cos_t, sin_t)
    ka_s[:, :HEAD_DIM] = kr.astype(BF16)
    ka_s[:, HEAD_DIM:] = jnp.where(blk_of_row == lane, 1.0, 0.0).astype(BF16)
    va_s[:, :HEAD_DIM] = v_ref[...]
    va_s[:, HEAD_DIM:] = jnp.ones((s_len, HEAD_DIM), BF16)
    km = jnp.mean(kr.reshape(n_blocks, MOBA_BLOCK, HEAD_DIM), axis=1)
    km = jnp.concatenate([km, jnp.zeros((LANES - n_blocks, HEAD_DIM), F32)], axis=0)

    qr = _rope(q_ref[...].astype(F32), cos_t, sin_t)
    q_hi, q_lo = _split_bf16(qr)
    km_hi, km_lo = _split_bf16(km)
    gate = _dot_nt(q_hi, km_hi) + _dot_nt(q_lo, km_hi) + _dot_nt(q_hi, km_lo)
    lane_f = lane.astype(F32)
    own_f = blk_of_row.astype(F32)
    g = jnp.where(lane_f < own_f, gate, GATE_MASKED)
    sel = jnp.where(lane_f == own_f, 1.0, 0.0)
    for _ in range(MOBA_TOPK):
        m = jnp.max(g, axis=1, keepdims=True)
        cand = jnp.where((g == m) & (g > 0.5 * GATE_MASKED), lane_f, float(LANES))
        idx = jnp.min(cand, axis=1, keepdims=True)
        pick = lane_f == idx
        sel = jnp.where(pick, 1.0, sel)
        g = jnp.where(pick, GATE_MASKED, g)
    qa_s[:, :HEAD_DIM] = (qr * scale).astype(BF16)
    qa_s[:, HEAD_DIM:] = jnp.where(sel > 0.5, 0.0, NEG).astype(BF16)

    rr = lax.broadcasted_iota(jnp.int32, (MOBA_BLOCK, MOBA_BLOCK), 0)
    cc = lax.broadcasted_iota(jnp.int32, (MOBA_BLOCK, MOBA_BLOCK), 1)
    causal = cc <= rr
    for t in range(n_blocks):
        r0 = t * MOBA_BLOCK
        n_keys = r0 + MOBA_BLOCK
        qa = qa_s[r0:n_keys, :]
        s_own = jnp.where(causal, _dot_nt(qa, ka_s[r0:n_keys, :]), NEG)
        if t > 0:
            s_all = jnp.concatenate([_dot_nt(qa, ka_s[0:r0, :]), s_own], axis=1)
        else:
            s_all = s_own
        m = jnp.max(s_all, axis=1, keepdims=True)
        p = jnp.exp2(s_all - m)
        acc = _dot(p.astype(BF16), va_s[0:n_keys, :])
        o = (acc[:, :HEAD_DIM] * (1.0 / acc[:, HEAD_DIM:])
             * _silu(z_ref[r0:n_keys, :].astype(F32)))
        o_ref[r0:n_keys, :] = o.astype(o_ref.dtype)


def _moba(proj3, cos_t, sin_t):
    b, s, _ = proj3.shape
    seq_spec = lambda col: pl.BlockSpec((None, s, HEAD_DIM), lambda bi, h: (bi, 0, col + h))
    table = pl.BlockSpec((s, LANES), lambda bi, h: (0, 0))
    return pl.pallas_call(
        _moba_kernel,
        grid=(b, N_HEADS),
        in_specs=[seq_spec(COL_QA), seq_spec(COL_KA), seq_spec(COL_VA), seq_spec(COL_ZA),
                  table, table],
        out_specs=pl.BlockSpec((None, s, HEAD_DIM), lambda bi, h: (bi, 0, h)),
        out_shape=jax.ShapeDtypeStruct((b, s, N_HEADS * HEAD_DIM), BF16),
        scratch_shapes=[
            pltpu.VMEM((s, 2 * HEAD_DIM), BF16),
            pltpu.VMEM((s, 2 * HEAD_DIM), BF16),
            pltpu.VMEM((s, 2 * HEAD_DIM), BF16),
        ],
        compiler_params=pltpu.CompilerParams(
            dimension_semantics=("arbitrary", "arbitrary"),
            vmem_limit_bytes=VMEM_LIMIT),
        name="moba_attention",
    )(proj3, proj3, proj3, proj3, cos_t, sin_t)


GDN_PAIR = 2 * GDN_CHUNK
GDN_PAD = 32
GDN_ROWS = 512


def _gdn_kernel(q_ref, k_ref, v_ref, z_ref, sm_ref, cwq_ref, cwk_ref, cwv_ref,
                alane_ref, dlane_ref, nw_ref, o_ref,
                pad_s, ball_s, gall_s, gt_s, b_s, g_s,
                qb_s, qe_s, kb_s, kbb_s, kd_s, rhs_s,
                m_s, x_s, qk_s, c_s, bb_s, qp_s, op_s, st_s):
    h = pl.program_id(1)
    s_len = q_ref.shape[0]
    c = GDN_CHUNK
    pp = GDN_PAIR
    n_chunks = s_len // c
    n_pairs = s_len // pp
    n_tiles = s_len // GDN_ROWS
    chunk_shift = c.bit_length() - 1

    def rows_of(i, size):
        return pl.ds(pl.multiple_of(i * size, size), size)

    @pl.when(h == 0)
    def _():
        pad_s[0:GDN_PAD, :] = jnp.zeros((GDN_PAD, LANES), F32)
        sm = sm_ref[...]
        ball_s[...] = _sigmoid(sm)
        xs = sm + dlane_ref[...]
        softplus = jnp.maximum(xs, 0.0) + jnp.log(1.0 + jnp.exp(-jnp.abs(xs)))
        pad_s[GDN_PAD:, :] = -jnp.exp(alane_ref[...]) * softplus
        pos = lax.broadcasted_iota(jnp.int32, (s_len, LANES), 0) & (c - 1)
        shift = 1
        while shift < c:
            cur = pad_s[GDN_PAD:, :]
            prev = pad_s[pl.ds(GDN_PAD - shift, s_len), :]
            pad_s[GDN_PAD:, :] = cur + jnp.where(pos >= shift, prev, 0.0)
            shift *= 2
        gall_s[...] = pad_s[GDN_PAD:, :]

        def tr(j, carry):
            gt_s[j] = gall_s[rows_of(j, pp), :].T
            return carry
        lax.fori_loop(0, n_pairs, tr, 0)

    def conv_silu_tile(i, cw):
        base = pl.multiple_of(i * GDN_ROWS, GDN_ROWS) + GDN_PAD
        y = pad_s[pl.ds(base, GDN_ROWS), :] * cw[CONV_WIDTH - 1:CONV_WIDTH, :]
        for back in range(1, CONV_WIDTH):
            y = y + (pad_s[pl.ds(base - back, GDN_ROWS), :]
                     * cw[CONV_WIDTH - 1 - back:CONV_WIDTH - back, :])
        return _silu(y)

    def l2n(x):
        return x * lax.rsqrt(jnp.sum(x * x, axis=-1, keepdims=True) + EPS)

    pad_s[GDN_PAD:, :] = k_ref[...].astype(F32)
    cwk = cwk_ref[...]
    lane_t = lax.broadcasted_iota(jnp.int32, (GDN_ROWS, LANES), 1)

    def k_tile(i, carry):
        r = rows_of(i, GDN_ROWS)
        bcol = jnp.sum(jnp.where(lane_t == h + LANE_BETA, ball_s[r, :], 0.0), axis=1, keepdims=True)
        gcol = jnp.sum(jnp.where(lane_t == h + LANE_DECAY, gall_s[r, :], 0.0), axis=1, keepdims=True)
        bt = jnp.broadcast_to(bcol, (GDN_ROWS, LANES))
        gc = jnp.broadcast_to(gcol, (GDN_ROWS, LANES))
        b_s[r, :] = bt
        g_s[r, :] = gc
        kn = l2n(conv_silu_tile(i, cwk))
        kbeta = kn * bt
        kb_s[r, :] = kn.astype(BF16)
        kbb_s[r, :] = kbeta.astype(BF16)
        rhs_s[r, 0:HEAD_DIM] = (kbeta * jnp.exp(gc)).astype(BF16)
        g3 = gc.reshape(GDN_ROWS // c, c, LANES)
        kd = kn.reshape(GDN_ROWS // c, c, LANES) * jnp.exp(g3[:, c - 1:c, :] - g3)
        kd_s[r, :] = kd.reshape(GDN_ROWS, LANES).astype(BF16)
        return carry
    lax.fori_loop(0, n_tiles, k_tile, 0)

    pad_s[GDN_PAD:, :] = q_ref[...].astype(F32)
    cwq = cwq_ref[...]

    def q_tile(i, carry):
        r = rows_of(i, GDN_ROWS)
        qn = l2n(conv_silu_tile(i, cwq)) * (HEAD_DIM ** -0.5)
        qb_s[r, :] = qn.astype(BF16)
        qe_s[r, :] = qn * jnp.exp(g_s[r, :])
        return carry
    lax.fori_loop(0, n_tiles, q_tile, 0)

    pad_s[GDN_PAD:, :] = v_ref[...].astype(F32)
    cwv = cwv_ref[...]

    def v_tile(i, carry):
        r = rows_of(i, GDN_ROWS)
        rhs_s[r, HEAD_DIM:2 * HEAD_DIM] = (conv_silu_tile(i, cwv) * b_s[r, :]).astype(BF16)
        return carry
    lax.fori_loop(0, n_tiles, v_tile, 0)

    ri = lax.broadcasted_iota(jnp.int32, (pp, pp), 0)
    ci = lax.broadcasted_iota(jnp.int32, (pp, pp), 1)
    same_chunk = _shr(ri, chunk_shift) == _shr(ci, chunk_shift)
    causal = same_chunk & (ci <= ri)
    strict_f = jnp.where(same_chunk & (ci < ri), 1.0, 0.0)
    eye = jnp.where(ci == ri, 1.0, 0.0)
    level_masks = []
    size = 1
    while size < c:
        sh = size.bit_length() - 1
        same_big = _shr(ri, sh + 1) == _shr(ci, sh + 1)
        diff_small = _shr(ri, sh) != _shr(ci, sh)
        level_masks.append(jnp.where(same_big & diff_small & (ci < ri), 1.0, 0.0))
        size *= 2

    def gram(j, carry):
        r = rows_of(j, pp)
        g_col = g_s[r, :]
        g_row = gt_s[j, pl.ds(h + LANE_DECAY, 1), :]
        diff = g_col - g_row
        decay = jnp.where(causal, jnp.exp(jnp.where(causal, diff, 0.0)), 0.0)
        kk = kb_s[r, :]
        m = _dot_nt(kbb_s[r, :], kk) * decay * strict_f
        qk_s[r, :] = (_dot_nt(qb_s[r, :], kk) * decay).astype(BF16)
        m_s[r, :] = m
        x_s[r, :] = eye - m * level_masks[0]
        return carry
    lax.fori_loop(0, n_pairs, gram, 0, unroll=8)

    for lm in level_masks[1:]:
        def level(j, carry, lm=lm):
            r = rows_of(j, pp)
            x = x_s[r, :]
            xb = x.astype(BF16)
            y = _dot(xb, (m_s[r, :] * lm).astype(BF16))
            x_s[r, :] = x - _dot(y.astype(BF16), xb)
            return carry
        lax.fori_loop(0, n_pairs, level, 0, unroll=8)

    def solve(j, carry):
        r = rows_of(j, pp)
        wu = _dot(x_s[r, :].astype(BF16), rhs_s[r, :]).astype(BF16)
        qo = _dot(qk_s[r, :], wu)
        qp_s[r, :] = (qe_s[r, :] - qo[:, 0:HEAD_DIM]).astype(BF16)
        op_s[r, :] = qo[:, HEAD_DIM:2 * HEAD_DIM]
        for half in range(2):
            rc = pl.ds(pl.multiple_of(j * pp + half * c, c), c)
            cb = _dot_tn(kd_s[rc, :], wu[half * c:(half + 1) * c, :])
            c_s[2 * j + half] = cb[:, 0:HEAD_DIM].astype(BF16)
            bb_s[2 * j + half] = cb[:, HEAD_DIM:2 * HEAD_DIM]
        return carry
    lax.fori_loop(0, n_pairs, solve, 0, unroll=8)

    def scan(n, state):
        sb = state.astype(BF16)
        st_s[n] = sb
        gl = jnp.exp(g_s[pl.ds(n * c + (c - 1), 1), :])
        return state * gl - _dot(c_s[n], sb) + bb_s[n]
    lax.fori_loop(0, n_chunks, scan, jnp.zeros((HEAD_DIM, HEAD_DIM), F32))

    nw = nw_ref[...]

    def emit(n, carry):
        r = rows_of(n, c)
        o = _dot(qp_s[r, :], st_s[n]) + op_s[r, :]
        o = o * lax.rsqrt(jnp.mean(o * o, axis=-1, keepdims=True) + EPS) * nw
        o_ref[r, :] = (o * _silu(z_ref[r, :].astype(F32))).astype(o_ref.dtype)
        return carry
    lax.fori_loop(0, n_chunks, emit, 0, unroll=8)


def _gdn(proj3, small3, conv_w, alane, dlane, gdn_norm_w):
    b, s, _ = proj3.shape
    c = GDN_CHUNK
    seq_spec = lambda col: pl.BlockSpec((None, s, HEAD_DIM), lambda bi, h: (bi, 0, col + h))
    cw_spec = lambda off: pl.BlockSpec((CONV_WIDTH, HEAD_DIM), lambda bi, h: (0, off + h))
    row_spec = pl.BlockSpec((1, LANES), lambda bi, h: (0, 0))
    seq = lambda dt, w=HEAD_DIM: pltpu.VMEM((s, w), dt)
    return pl.pallas_call(
        _gdn_kernel,
        grid=(b, N_HEADS),
        in_specs=[
            seq_spec(COL_QB), seq_spec(COL_KB), seq_spec(COL_VB), seq_spec(COL_ZB),
            pl.BlockSpec((None, s, LANES), lambda bi, h: (bi, 0, 0)),
            cw_spec(0), cw_spec(N_HEADS), cw_spec(2 * N_HEADS),
            row_spec, row_spec, row_spec,
        ],
        out_specs=pl.BlockSpec((None, s, HEAD_DIM), lambda bi, h: (bi, 0, h)),
        out_shape=jax.ShapeDtypeStruct((b, s, N_HEADS * HEAD_DIM), BF16),
        scratch_shapes=[
            pltpu.VMEM((s + GDN_PAD, LANES), F32),
            seq(F32), seq(F32),
            pltpu.VMEM((s // GDN_PAIR, LANES, GDN_PAIR), F32),
            seq(F32), seq(F32),
            seq(BF16), seq(F32),
            seq(BF16), seq(BF16), seq(BF16),
            seq(BF16, 2 * HEAD_DIM),
            seq(F32), seq(F32), seq(BF16),
            pltpu.VMEM((s // c, HEAD_DIM, HEAD_DIM), BF16),
            pltpu.VMEM((s // c, HEAD_DIM, HEAD_DIM), F32),
            seq(BF16), seq(F32),
            pltpu.VMEM((s // c, HEAD_DIM, HEAD_DIM), BF16),
        ],
        compiler_params=pltpu.CompilerParams(
            dimension_semantics=("arbitrary", "arbitrary"),
            vmem_limit_bytes=VMEM_LIMIT),
        name="gated_deltanet",
    )(proj3, proj3, proj3, proj3, small3, conv_w, conv_w, conv_w, alane, dlane, gdn_norm_w)


def _out_kernel(x_ref, a_ref, b_ref, ga_ref, gb_ref, wa_ref, wb_ref, wo_ref, nw_ref, o_ref):
    y_a = _dot(a_ref[...], wa_ref[...])
    y_b = _dot(b_ref[...], wb_ref[...])
    merged = (_sigmoid(ga_ref[...].astype(F32)) * y_a
              + _sigmoid(gb_ref[...].astype(F32)) * y_b)
    out = _dot(merged.astype(BF16), wo_ref[...])
    ms = jnp.mean(out * out, axis=-1, keepdims=True)
    o_ref[...] = x_ref[...] + out * lax.rsqrt(ms + EPS) * nw_ref[...]


def _out_projection(x2, a2, b2, proj2, wa, wb, wo, post_norm_w, tm=512):
    t = x2.shape[0]
    tok = lambda: pl.BlockSpec((tm, D_MODEL), lambda i: (i, 0))
    full = lambda: pl.BlockSpec((D_MODEL, D_MODEL), lambda i: (0, 0))
    return pl.pallas_call(
        _out_kernel,
        grid=(t // tm,),
        in_specs=[
            tok(), tok(), tok(),
            pl.BlockSpec((tm, D_MODEL), lambda i: (i, BLK_GATE_A)),
            pl.BlockSpec((tm, D_MODEL), lambda i: (i, BLK_GATE_B)),
            full(), full(), full(),
            pl.BlockSpec((1, D_MODEL), lambda i: (0, 0)),
        ],
        out_specs=tok(),
        out_shape=jax.ShapeDtypeStruct((t, D_MODEL), F32),
        compiler_params=pltpu.CompilerParams(
            dimension_semantics=("arbitrary",),
            vmem_limit_bytes=VMEM_LIMIT),
        name="out_projection",
    )(x2, a2, b2, proj2, proj2, wa, wb, wo, post_norm_w)


def _rope_tables(s):
    half = ROPE_DIM // 2
    inv_freq = jnp.power(ROPE_THETA, -jnp.arange(half, dtype=F32) * (2.0 / ROPE_DIM))
    ang = jnp.arange(s, dtype=F32)[:, None] * inv_freq[None, :]
    cos, sin = jnp.cos(ang), jnp.sin(ang)
    ones = jnp.ones((s, HEAD_DIM - ROPE_DIM), F32)
    zeros = jnp.zeros((s, HEAD_DIM - ROPE_DIM), F32)
    cos_t = jnp.concatenate([cos, cos, ones], axis=1)
    sin_t = jnp.concatenate([-sin, sin, zeros], axis=1)
    return cos_t, sin_t


def _head_lanes(v):
    return jnp.pad(v.astype(F32), (LANE_DECAY, LANES - LANE_DECAY - N_HEADS)).reshape(1, LANES)


def _layer(x, pre_norm_w, w_in, conv_w, a_log, dt_bias, gdn_norm_w,
           w_branch_a, w_branch_b, w_out, post_norm_w):
    b, s, d = x.shape
    x2 = x.reshape(b * s, d)
    n_main = 8 * D_MODEL
    w_main = jnp.concatenate([w_in[:, :n_main], w_in[:, n_main + 2 * N_HEADS:]], axis=1).astype(BF16)
    w_small = jnp.pad(w_in[:, n_main:n_main + 2 * N_HEADS], ((0, 0), (0, LANES - 2 * N_HEADS)))
    ws_hi = w_small.astype(BF16)
    ws_lo = (w_small - ws_hi.astype(F32)).astype(BF16)

    proj, small = _in_projection(x2, pre_norm_w.reshape(1, d), w_main, ws_hi, ws_lo)
    proj3 = proj.reshape(b, s, N_MAIN_BLOCKS * D_MODEL)
    small3 = small.reshape(b, s, LANES)

    cos_t, sin_t = _rope_tables(s)
    act_a = _moba(proj3, cos_t, sin_t)
    act_b = _gdn(proj3, small3, conv_w, _head_lanes(a_log), _head_lanes(dt_bias),
                 gdn_norm_w.reshape(1, HEAD_DIM))

    out = _out_projection(x2, act_a.reshape(b * s, d), act_b.reshape(b * s, d), proj,
                          w_branch_a.astype(BF16), w_branch_b.astype(BF16), w_out.astype(BF16),
                          post_norm_w.reshape(1, d))
    return out.reshape(b, s, d)


def kernel(x, pre_norm_w, w_in, conv_w, a_log, dt_bias, gdn_norm_w,
           w_branch_a, w_branch_b, w_out, post_norm_w):
    for layer in range(pre_norm_w.shape[0]):
        x = _layer(x, pre_norm_w[layer], w_in[layer], conv_w[layer], a_log[layer],
                   dt_bias[layer], gdn_norm_w[layer], w_branch_a[layer],
                   w_branch_b[layer], w_out[layer], post_norm_w[layer])
    return x
```

```python
import jax
import jax.numpy as jnp
from jax import lax
from jax.experimental import pallas as pl
from jax.experimental.pallas import tpu as pltpu

F32 = jnp.float32
BF16 = jnp.bfloat16

D_MODEL = 1024
N_HEADS = 8
HEAD_DIM = 128
MOBA_BLOCK = 256
MOBA_TOPK = 3
ROPE_THETA = 500000.0
ROPE_DIM = HEAD_DIM // 4
CONV_WIDTH = 4
GDN_CHUNK = 64
EPS = 1e-6
NEG = -1e30
GATE_MASKED = -3.0e38
LANES = 128
N_MAIN_BLOCKS = 10

COL_QA, COL_KA, COL_VA, COL_ZA = 0, 8, 16, 24
COL_QB, COL_KB, COL_VB, COL_ZB = 32, 40, 48, 56
BLK_GATE_A, BLK_GATE_B = 8, 9
LANE_BETA, LANE_DECAY = 0, N_HEADS

VMEM_LIMIT = 56 * 1024 * 1024
HIGHEST = lax.Precision.HIGHEST


def _dot(a, b, precision=None):
    return jnp.dot(a, b, preferred_element_type=F32, precision=precision)


def _dot_nt(a, b, precision=None):
    return lax.dot_general(a, b, (((1,), (1,)), ((), ())),
                           preferred_element_type=F32, precision=precision)


def _dot_tn(a, b, precision=None):
    return lax.dot_general(a, b, (((0,), (0,)), ((), ())),
                           preferred_element_type=F32, precision=precision)


def _silu(x):
    return x * (1.0 / (1.0 + jnp.exp(-x)))


def _sigmoid(x):
    return 1.0 / (1.0 + jnp.exp(-x))


def _shr(x, n):
    return lax.shift_right_logical(x, n)


def _proj_kernel(x_ref, nw_ref, w_ref, wsh_ref, wsl_ref, o_ref, os_ref, h_ref):
    j = pl.program_id(1)

    @pl.when(j == 0)
    def _():
        x = x_ref[...]
        ms = jnp.mean(x * x, axis=-1, keepdims=True)
        h = x * lax.rsqrt(ms + EPS) * nw_ref[...]
        h_hi = h.astype(BF16)
        h_lo = (h - h_hi.astype(F32)).astype(BF16)
        h_ref[...] = h_hi
        os_ref[...] = (_dot(h_hi, wsh_ref[...]) + _dot(h_lo, wsh_ref[...])
                       + _dot(h_hi, wsl_ref[...]))

    o_ref[...] = _dot(h_ref[...], w_ref[...]).astype(o_ref.dtype)


def _in_projection(x2, pre_norm_w, w_main, ws_hi, ws_lo, tm=2048):
    t = x2.shape[0]
    return pl.pallas_call(
        _proj_kernel,
        grid=(t // tm, N_MAIN_BLOCKS),
        in_specs=[
            pl.BlockSpec((tm, D_MODEL), lambda i, j: (i, 0)),
            pl.BlockSpec((1, D_MODEL), lambda i, j: (0, 0)),
            pl.BlockSpec((D_MODEL, D_MODEL), lambda i, j: (0, j)),
            pl.BlockSpec((D_MODEL, LANES), lambda i, j: (0, 0)),
            pl.BlockSpec((D_MODEL, LANES), lambda i, j: (0, 0)),
        ],
        out_specs=[
            pl.BlockSpec((tm, D_MODEL), lambda i, j: (i, j)),
            pl.BlockSpec((tm, LANES), lambda i, j: (i, 0)),
        ],
        out_shape=[
            jax.ShapeDtypeStruct((t, N_MAIN_BLOCKS * D_MODEL), BF16),
            jax.ShapeDtypeStruct((t, LANES), F32),
        ],
        scratch_shapes=[pltpu.VMEM((tm, D_MODEL), BF16)],
        compiler_params=pltpu.CompilerParams(
            dimension_semantics=("arbitrary", "arbitrary"),
            vmem_limit_bytes=VMEM_LIMIT),
        name="in_projection",
    )(x2, pre_norm_w, w_main, ws_hi, ws_lo)


def _rope(x, cos_t, sin_t):
    half = ROPE_DIM // 2
    lane = lax.broadcasted_iota(jnp.int32, x.shape, 1)
    partner = jnp.where(lane < half, pltpu.roll(x, LANES - half, 1), pltpu.roll(x, half, 1))
    return x * cos_t + partner * sin_t


def _split_bf16(x):
    hi = x.astype(BF16)
    return hi, (x - hi.astype(F32)).astype(BF16)


def _moba_kernel(q_ref, k_ref, v_ref, z_ref, cos_ref, sin_ref, o_ref, qa_s, ka_s, va_s):
    s_len = k_ref.shape[0]
    n_blocks = s_len // MOBA_BLOCK
    blk_shift = MOBA_BLOCK.bit_length() - 1
    scale = HEAD_DIM ** -0.5 * 1.4426950408889634
    cos_t = cos_ref[...]
    sin_t = sin_ref[...]
    row = lax.broadcasted_iota(jnp.int32, (s_len, LANES), 0)
    lane = lax.broadcasted_iota(jnp.int32, (s_len, LANES), 1)
    blk_of_row = _shr(row, blk_shift)

    kr = _rope(k_ref[...].astype(F32), cos_t, sin_t)
    ka_s[:, :HEAD_DIM] = kr.astype(BF16)
    ka_s[:, HEAD_DIM:] = jnp.where(blk_of_row == lane, 1.0, 0.0).astype(BF16)
    va_s[:, :HEAD_DIM] = v_ref[...]
    va_s[:, HEAD_DIM:] = jnp.ones((s_len, HEAD_DIM), BF16)
    km = jnp.mean(kr.reshape(n_blocks, MOBA_BLOCK, HEAD_DIM), axis=1)
    km = jnp.concatenate([km, jnp.zeros((LANES - n_blocks, HEAD_DIM), F32)], axis=0)

    qr = _rope(q_ref[...].astype(F32), cos_t, sin_t)
    q_hi, q_lo = _split_bf16(qr)
    km_hi, km_lo = _split_bf16(km)
    gate = _dot_nt(q_hi, km_hi) + _dot_nt(q_lo, km_hi) + _dot_nt(q_hi, km_lo)
    lane_f = lane.astype(F32)
    own_f = blk_of_row.astype(F32)
    g = jnp.where(lane_f < own_f, gate, GATE_MASKED)
    sel = jnp.where(lane_f == own_f, 1.0, 0.0)
    for _ in range(MOBA_TOPK):
        m = jnp.max(g, axis=1, keepdims=True)
        cand = jnp.where((g == m) & (g > 0.5 * GATE_MASKED), lane_f, float(LANES))
        idx = jnp.min(cand, axis=1, keepdims=True)
        pick = lane_f == idx
        sel = jnp.where(pick, 1.0, sel)
        g = jnp.where(pick, GATE_MASKED, g)
    qa_s[:, :HEAD_DIM] = (qr * scale).astype(BF16)
    qa_s[:, HEAD_DIM:] = jnp.where(sel > 0.5, 0.0, NEG).astype(BF16)

    rr = lax.broadcasted_iota(jnp.int32, (MOBA_BLOCK, MOBA_BLOCK), 0)
    cc = lax.broadcasted_iota(jnp.int32, (MOBA_BLOCK, MOBA_BLOCK), 1)
    causal = cc <= rr
    for t in range(n_blocks):
        r0 = t * MOBA_BLOCK
        n_keys = r0 + MOBA_BLOCK
        qa = qa_s[r0:n_keys, :]
        s_own = jnp.where(causal, _dot_nt(qa, ka_s[r0:n_keys, :]), NEG)
        if t > 0:
            s_all = jnp.concatenate([_dot_nt(qa, ka_s[0:r0, :]), s_own], axis=1)
        else:
            s_all = s_own
        m = jnp.max(s_all, axis=1, keepdims=True)
        p = jnp.exp2(s_all - m)
        acc = _dot(p.astype(BF16), va_s[0:n_keys, :])
        o = (acc[:, :HEAD_DIM] * (1.0 / acc[:, HEAD_DIM:])
             * _silu(z_ref[r0:n_keys, :].astype(F32)))
        o_ref[r0:n_keys, :] = o.astype(o_ref.dtype)


def _moba(proj3, cos_t, sin_t):
    b, s, _ = proj3.shape
    seq_spec = lambda col: pl.BlockSpec((None, s, HEAD_DIM), lambda bi, h: (bi, 0, col + h))
    table = pl.BlockSpec((s, LANES), lambda bi, h: (0, 0))
    return pl.pallas_call(
        _moba_kernel,
        grid=(b, N_HEADS),
        in_specs=[seq_spec(COL_QA), seq_spec(COL_KA), seq_spec(COL_VA), seq_spec(COL_ZA),
                  table, table],
        out_specs=pl.BlockSpec((None, s, HEAD_DIM), lambda bi, h: (bi, 0, h)),
        out_shape=jax.ShapeDtypeStruct((b, s, N_HEADS * HEAD_DIM), BF16),
        scratch_shapes=[
            pltpu.VMEM((s, 2 * HEAD_DIM), BF16),
            pltpu.VMEM((s, 2 * HEAD_DIM), BF16),
            pltpu.VMEM((s, 2 * HEAD_DIM), BF16),
        ],
        compiler_params=pltpu.CompilerParams(
            dimension_semantics=("arbitrary", "arbitrary"),
            vmem_limit_bytes=VMEM_LIMIT),
        name="moba_attention",
    )(proj3, proj3, proj3, proj3, cos_t, sin_t)


GDN_PAIR = 2 * GDN_CHUNK
GDN_PAD = 32
GDN_ROWS = 512
GDN_SCAN_GROUP = 4
GDN_UNROLL = 16
GDN_UNROLL_LEVELS = 32


def _gdn_kernel(q_ref, k_ref, v_ref, z_ref, sm_ref, cwq_ref, cwk_ref, cwv_ref,
                alane_ref, dlane_ref, nw_ref, o_ref,
                pad_s, ball_s, gall_s, gt_s, b_s, g_s,
                qb_s, qe_s, kb_s, kbb_s, kdt_s, rhs_s,
                m_s, x_s, qk_s, c_s, bb_s, qp_s, op_s, st_s, gm_s, e_s):
    h = pl.program_id(1)
    s_len = q_ref.shape[0]
    c = GDN_CHUNK
    pp = GDN_PAIR
    n_chunks = s_len // c
    n_pairs = s_len // pp
    n_tiles = s_len // GDN_ROWS
    chunk_shift = c.bit_length() - 1

    def rows_of(i, size):
        return pl.ds(pl.multiple_of(i * size, size), size)

    @pl.when(h == 0)
    def _():
        pad_s[0:GDN_PAD, :] = jnp.zeros((GDN_PAD, LANES), F32)
        sm = sm_ref[...]
        ball_s[...] = _sigmoid(sm)
        xs = sm + dlane_ref[...]
        softplus = jnp.maximum(xs, 0.0) + jnp.log(1.0 + jnp.exp(-jnp.abs(xs)))
        pad_s[GDN_PAD:, :] = -jnp.exp(alane_ref[...]) * softplus
        pos = lax.broadcasted_iota(jnp.int32, (s_len, LANES), 0) & (c - 1)
        shift = 1
        while shift < c:
            cur = pad_s[GDN_PAD:, :]
            prev = pad_s[pl.ds(GDN_PAD - shift, s_len), :]
            pad_s[GDN_PAD:, :] = cur + jnp.where(pos >= shift, prev, 0.0)
            shift *= 2
        gall_s[...] = pad_s[GDN_PAD:, :]

        def tr(j, carry):
            gt_s[j] = gall_s[rows_of(j, pp), :].T
            return carry
        lax.fori_loop(0, n_pairs, tr, 0)

    def conv_silu_tile(i, cw):
        base = pl.multiple_of(i * GDN_ROWS, GDN_ROWS) + GDN_PAD
        y = pad_s[pl.ds(base, GDN_ROWS), :] * cw[CONV_WIDTH - 1:CONV_WIDTH, :]
        for back in range(1, CONV_WIDTH):
            y = y + (pad_s[pl.ds(base - back, GDN_ROWS), :]
                     * cw[CONV_WIDTH - 1 - back:CONV_WIDTH - back, :])
        return _silu(y)

    def l2n(x):
        return x * lax.rsqrt(jnp.sum(x * x, axis=-1, keepdims=True) + EPS)

    lane_t = lax.broadcasted_iota(jnp.int32, (GDN_ROWS, LANES), 1)

    def head_column(all_s, r, lane_idx):
        col = jnp.sum(jnp.where(lane_t == lane_idx, all_s[r, :], 0.0), axis=1, keepdims=True)
        return jnp.broadcast_to(col, (GDN_ROWS, LANES))

    pad_s[GDN_PAD:, :] = v_ref[...].astype(F32)
    cwv = cwv_ref[...]

    def v_tile(i, carry):
        r = rows_of(i, GDN_ROWS)
        bt = head_column(ball_s, r, h + LANE_BETA)
        b_s[r, :] = bt
        rhs_s[r, HEAD_DIM:2 * HEAD_DIM] = (conv_silu_tile(i, cwv) * bt).astype(BF16)
        return carry
    lax.fori_loop(0, n_tiles, v_tile, 0)

    pad_s[GDN_PAD:, :] = k_ref[...].astype(F32)
    cwk = cwk_ref[...]

    def k_tile(i, carry):
        r = rows_of(i, GDN_ROWS)
        gc = head_column(gall_s, r, h + LANE_DECAY)
        g_s[r, :] = gc
        kn = l2n(conv_silu_tile(i, cwk))
        kbeta = kn * b_s[r, :]
        kb_s[r, :] = kn.astype(BF16)
        kbb_s[r, :] = kbeta.astype(BF16)
        rhs_s[r, 0:HEAD_DIM] = (kbeta * jnp.exp(gc)).astype(BF16)
        g3 = gc.reshape(GDN_ROWS // c, c, LANES)
        kd = (kn.reshape(GDN_ROWS // c, c, LANES) * jnp.exp(g3[:, c - 1:c, :] - g3)
              ).reshape(GDN_ROWS, LANES)
        for p in range(GDN_ROWS // pp):
            kdt_s[i * (GDN_ROWS // pp) + p] = kd[p * pp:(p + 1) * pp, :].T.astype(BF16)
        return carry
    lax.fori_loop(0, n_tiles, k_tile, 0)

    pad_s[GDN_PAD:, :] = q_ref[...].astype(F32)
    cwq = cwq_ref[...]

    def q_tile(i, carry):
        r = rows_of(i, GDN_ROWS)
        qn = l2n(conv_silu_tile(i, cwq)) * (HEAD_DIM ** -0.5)
        qb_s[r, :] = qn.astype(BF16)
        qe_s[r, :] = qn * jnp.exp(g_s[r, :])
        return carry
    lax.fori_loop(0, n_tiles, q_tile, 0)

    ri = lax.broadcasted_iota(jnp.int32, (pp, pp), 0)
    ci = lax.broadcasted_iota(jnp.int32, (pp, pp), 1)
    same_chunk = _shr(ri, chunk_shift) == _shr(ci, chunk_shift)
    causal = same_chunk & (ci <= ri)
    strict_f = jnp.where(same_chunk & (ci < ri), 1.0, 0.0)
    eye = jnp.where(ci == ri, 1.0, 0.0)
    level_masks = []
    size = 1
    while size < c:
        sh = size.bit_length() - 1
        same_big = _shr(ri, sh + 1) == _shr(ci, sh + 1)
        diff_small = _shr(ri, sh) != _shr(ci, sh)
        level_masks.append(jnp.where(same_big & diff_small & (ci < ri), 1.0, 0.0))
        size *= 2

    def gram(j, carry):
        r = rows_of(j, pp)
        g_col = g_s[r, :]
        g_row = gt_s[j, pl.ds(h + LANE_DECAY, 1), :]
        diff = g_col - g_row
        decay = jnp.where(causal, jnp.exp(jnp.where(causal, diff, 0.0)), 0.0)
        kk = kb_s[r, :]
        m = _dot_nt(kbb_s[r, :], kk) * decay * strict_f
        qk_s[r, :] = (_dot_nt(qb_s[r, :], kk) * decay).astype(BF16)
        m_s[r, :] = m
        x_s[r, :] = eye - m * level_masks[0]
        return carry
    lax.fori_loop(0, n_pairs, gram, 0, unroll=min(n_pairs, GDN_UNROLL))

    size = 1
    for lm in level_masks[1:]:
        size *= 2
        if size % 8 == 0:
            def level(j, carry, lm=lm, size=size):
                base = pl.multiple_of(j * pp, pp)
                r = pl.ds(base, pp)
                odd_rows = [pl.ds(base + (2 * b + 1) * size, size) for b in range(pp // (2 * size))]
                xo = jnp.concatenate([x_s[rr, :] for rr in odd_rows], axis=0)
                y = _dot(xo.astype(BF16), (m_s[r, :] * lm).astype(BF16))
                new = xo - _dot(y.astype(BF16), x_s[r, :].astype(BF16))
                for b, rr in enumerate(odd_rows):
                    x_s[rr, :] = new[b * size:(b + 1) * size, :]
                return carry
        else:
            def level(j, carry, lm=lm):
                r = rows_of(j, pp)
                x = x_s[r, :]
                xb = x.astype(BF16)
                y = _dot(xb, (m_s[r, :] * lm).astype(BF16))
                x_s[r, :] = x - _dot(y.astype(BF16), xb)
                return carry
        lax.fori_loop(0, n_pairs, level, 0, unroll=min(n_pairs, GDN_UNROLL_LEVELS))

    def solve(j, carry):
        r = rows_of(j, pp)
        wu = _dot(x_s[r, :].astype(BF16), rhs_s[r, :]).astype(BF16)
        qo = _dot(qk_s[r, :], wu)
        qp_s[r, :] = (qe_s[r, :] - qo[:, 0:HEAD_DIM]).astype(BF16)
        op_s[r, :] = qo[:, HEAD_DIM:2 * HEAD_DIM]
        kdt = kdt_s[j]
        for half in range(2):
            cb = _dot(kdt[:, half * c:(half + 1) * c], wu[half * c:(half + 1) * c, :])
            c_s[2 * j + half] = (-cb[:, 0:HEAD_DIM]).astype(BF16)
            bb_s[2 * j + half] = cb[:, HEAD_DIM:2 * HEAD_DIM]
        return carry
    lax.fori_loop(0, n_pairs, solve, 0, unroll=min(n_pairs, GDN_UNROLL))

    grp = GDN_SCAN_GROUP
    n_groups = n_chunks // grp

    def glast_row(n):
        return g_s[pl.ds(n * c + (c - 1), 1), :]

    def compose(gi, carry):
        n0 = gi * grp
        gsum = glast_row(n0)
        gm_s[pl.ds(n0, 1), :] = jnp.exp(gsum)
        p_mat = c_s[n0].astype(F32)
        q_mat = bb_s[n0]
        for jj in range(1, grp):
            n = n0 + jj
            glj = glast_row(n)
            n_mat = c_s[n]
            res = _dot(n_mat, jnp.concatenate([p_mat.astype(BF16), q_mat.astype(BF16)], axis=1))
            p_mat = jnp.exp(glj) * p_mat + jnp.exp(gsum) * n_mat.astype(F32) + res[:, 0:HEAD_DIM]
            q_mat = jnp.exp(glj) * q_mat + res[:, HEAD_DIM:2 * HEAD_DIM] + bb_s[n]
            gsum = gsum + glj
            c_s[n] = p_mat.astype(BF16)
            bb_s[n] = q_mat
            gm_s[pl.ds(n, 1), :] = jnp.exp(gsum)
        return carry
    lax.fori_loop(0, n_groups, compose, 0, unroll=min(n_groups, GDN_UNROLL))

    def chain(gi, e):
        eb = e.astype(BF16)
        e_s[gi] = e
        st_s[gi * grp] = eb
        nl = gi * grp + (grp - 1)
        return e * gm_s[pl.ds(nl, 1), :] + _dot(c_s[nl], eb) + bb_s[nl]
    lax.fori_loop(0, n_groups, chain, jnp.zeros((HEAD_DIM, HEAD_DIM), F32))

    def fill(gi, carry):
        e = e_s[gi]
        eb = st_s[gi * grp]
        for jj in range(1, grp):
            n = gi * grp + jj
            st_s[n] = (e * gm_s[pl.ds(n - 1, 1), :] + _dot(c_s[n - 1], eb) + bb_s[n - 1]).astype(BF16)
        return carry
    lax.fori_loop(0, n_groups, fill, 0, unroll=min(n_groups, GDN_UNROLL))

    nw = nw_ref[...]

    def emit(n, carry):
        r = rows_of(n, c)
        o = _dot(qp_s[r, :], st_s[n]) + op_s[r, :]
        o = o * lax.rsqrt(jnp.mean(o * o, axis=-1, keepdims=True) + EPS) * nw
        o_ref[r, :] = (o * _silu(z_ref[r, :].astype(F32))).astype(o_ref.dtype)
        return carry
    lax.fori_loop(0, n_chunks, emit, 0, unroll=min(n_chunks, GDN_UNROLL))


def _gdn(proj3, small3, conv_w, alane, dlane, gdn_norm_w):
    b, s, _ = proj3.shape
    c = GDN_CHUNK
    seq_spec = lambda col: pl.BlockSpec((None, s, HEAD_DIM), lambda bi, h: (bi, 0, col + h))
    cw_spec = lambda off: pl.BlockSpec((CONV_WIDTH, HEAD_DIM), lambda bi, h: (0, off + h))
    row_spec = pl.BlockSpec((1, LANES), lambda bi, h: (0, 0))
    seq = lambda dt, w=HEAD_DIM: pltpu.VMEM((s, w), dt)
    return pl.pallas_call(
        _gdn_kernel,
        grid=(b, N_HEADS),
        in_specs=[
            seq_spec(COL_QB), seq_spec(COL_KB), seq_spec(COL_VB), seq_spec(COL_ZB),
            pl.BlockSpec((None, s, LANES), lambda bi, h: (bi, 0, 0)),
            cw_spec(0), cw_spec(N_HEADS), cw_spec(2 * N_HEADS),
            row_spec, row_spec, row_spec,
        ],
        out_specs=pl.BlockSpec((None, s, HEAD_DIM), lambda bi, h: (bi, 0, h)),
        out_shape=jax.ShapeDtypeStruct((b, s, N_HEADS * HEAD_DIM), BF16),
        scratch_shapes=[
            pltpu.VMEM((s + GDN_PAD, LANES), F32),
            seq(F32), seq(F32),
            pltpu.VMEM((s // GDN_PAIR, LANES, GDN_PAIR), F32),
            seq(F32), seq(F32),
            seq(BF16), seq(F32),
            seq(BF16), seq(BF16),
            pltpu.VMEM((s // GDN_PAIR, HEAD_DIM, GDN_PAIR), BF16),
            seq(BF16, 2 * HEAD_DIM),
            seq(F32), seq(F32), seq(BF16),
            pltpu.VMEM((s // c, HEAD_DIM, HEAD_DIM), BF16),
            pltpu.VMEM((s // c, HEAD_DIM, HEAD_DIM), F32),
            seq(BF16), seq(F32),
            pltpu.VMEM((s // c, HEAD_DIM, HEAD_DIM), BF16),
            pltpu.VMEM((s // c, LANES), F32),
            pltpu.VMEM((s // c // GDN_SCAN_GROUP, HEAD_DIM, HEAD_DIM), F32),
        ],
        compiler_params=pltpu.CompilerParams(
            dimension_semantics=("arbitrary", "arbitrary"),
            vmem_limit_bytes=VMEM_LIMIT),
        name="gated_deltanet",
    )(proj3, proj3, proj3, proj3, small3, conv_w, conv_w, conv_w, alane, dlane, gdn_norm_w)


def _out_kernel(x_ref, a_ref, b_ref, ga_ref, gb_ref, wa_ref, wb_ref, wo_ref, nw_ref, o_ref):
    y_a = _dot(a_ref[...], wa_ref[...])
    y_b = _dot(b_ref[...], wb_ref[...])
    merged = (_sigmoid(ga_ref[...].astype(F32)) * y_a
              + _sigmoid(gb_ref[...].astype(F32)) * y_b)
    out = _dot(merged.astype(BF16), wo_ref[...])
    ms = jnp.mean(out * out, axis=-1, keepdims=True)
    o_ref[...] = x_ref[...] + out * lax.rsqrt(ms + EPS) * nw_ref[...]


def _out_projection(x2, a2, b2, proj2, wa, wb, wo, post_norm_w, tm=512):
    t = x2.shape[0]
    tok = lambda: pl.BlockSpec((tm, D_MODEL), lambda i: (i, 0))
    full = lambda: pl.BlockSpec((D_MODEL, D_MODEL), lambda i: (0, 0))
    return pl.pallas_call(
        _out_kernel,
        grid=(t // tm,),
        in_specs=[
            tok(), tok(), tok(),
            pl.BlockSpec((tm, D_MODEL), lambda i: (i, BLK_GATE_A)),
            pl.BlockSpec((tm, D_MODEL), lambda i: (i, BLK_GATE_B)),
            full(), full(), full(),
            pl.BlockSpec((1, D_MODEL), lambda i: (0, 0)),
        ],
        out_specs=tok(),
        out_shape=jax.ShapeDtypeStruct((t, D_MODEL), F32),
        compiler_params=pltpu.CompilerParams(
            dimension_semantics=("arbitrary",),
            vmem_limit_bytes=VMEM_LIMIT),
        name="out_projection",
    )(x2, a2, b2, proj2, proj2, wa, wb, wo, post_norm_w)


def _rope_tables(s):
    half = ROPE_DIM // 2
    inv_freq = jnp.power(ROPE_THETA, -jnp.arange(half, dtype=F32) * (2.0 / ROPE_DIM))
    ang = jnp.arange(s, dtype=F32)[:, None] * inv_freq[None, :]
    cos, sin = jnp.cos(ang), jnp.sin(ang)
    ones = jnp.ones((s, HEAD_DIM - ROPE_DIM), F32)
    zeros = jnp.zeros((s, HEAD_DIM - ROPE_DIM), F32)
    cos_t = jnp.concatenate([cos, cos, ones], axis=1)
    sin_t = jnp.concatenate([-sin, sin, zeros], axis=1)
    return cos_t, sin_t


def _head_lanes(v):
    return jnp.pad(v.astype(F32), (LANE_DECAY, LANES - LANE_DECAY - N_HEADS)).reshape(1, LANES)


def _layer(x, pre_norm_w, w_in, conv_w, a_log, dt_bias, gdn_norm_w,
           w_branch_a, w_branch_b, w_out, post_norm_w):
    b, s, d = x.shape
    x2 = x.reshape(b * s, d)
    n_main = 8 * D_MODEL
    w_main = jnp.concatenate([w_in[:, :n_main], w_in[:, n_main + 2 * N_HEADS:]], axis=1).astype(BF16)
    w_small = jnp.pad(w_in[:, n_main:n_main + 2 * N_HEADS], ((0, 0), (0, LANES - 2 * N_HEADS)))
    ws_hi = w_small.astype(BF16)
    ws_lo = (w_small - ws_hi.astype(F32)).astype(BF16)

    proj, small = _in_projection(x2, pre_norm_w.reshape(1, d), w_main, ws_hi, ws_lo)
    proj3 = proj.reshape(b, s, N_MAIN_BLOCKS * D_MODEL)
    small3 = small.reshape(b, s, LANES)

    cos_t, sin_t = _rope_tables(s)
    act_a = _moba(proj3, cos_t, sin_t)
    act_b = _gdn(proj3, small3, conv_w, _head_lanes(a_log), _head_lanes(dt_bias),
                 gdn_norm_w.reshape(1, HEAD_DIM))

    out = _out_projection(x2, act_a.reshape(b * s, d), act_b.reshape(b * s, d), proj,
                          w_branch_a.astype(BF16), w_branch_b.astype(BF16), w_out.astype(BF16),
                          post_norm_w.reshape(1, d))
    return out.reshape(b, s, d)


def kernel(x, pre_norm_w, w_in, conv_w, a_log, dt_bias, gdn_norm_w,
           w_branch_a, w_branch_b, w_out, post_norm_w):
    for layer in range(pre_norm_w.shape[0]):
        x = _layer(x, pre_norm_w[layer], w_in[layer], conv_w[layer], a_log[layer],
                   dt_bias[layer], gdn_norm_w[layer], w_branch_a[layer],
                   w_branch_b[layer], w_out[layer], post_norm_w[layer])
    return x
```

```python
import jax
import jax.numpy as jnp
from jax import lax
from jax.experimental import pallas as pl
from jax.experimental.pallas import tpu as pltpu

F32 = jnp.float32
BF16 = jnp.bfloat16

D_MODEL = 1024
N_HEADS = 8
HEAD_DIM = 128
MOBA_BLOCK = 256
MOBA_TOPK = 3
ROPE_THETA = 500000.0
ROPE_DIM = HEAD_DIM // 4
CONV_WIDTH = 4
GDN_CHUNK = 64
EPS = 1e-6
NEG = -1e30
GATE_MASKED = -3.0e38
LANES = 128
N_MAIN_BLOCKS = 10

COL_QA, COL_KA, COL_VA, COL_ZA = 0, 8, 16, 24
COL_QB, COL_KB, COL_VB, COL_ZB = 32, 40, 48, 56
BLK_GATE_A, BLK_GATE_B = 8, 9
LANE_BETA, LANE_DECAY = 0, N_HEADS

VMEM_LIMIT = 58 * 1024 * 1024
HIGHEST = lax.Precision.HIGHEST


def _dot(a, b, precision=None):
    return jnp.dot(a, b, preferred_element_type=F32, precision=precision)


def _dot_nt(a, b, precision=None):
    return lax.dot_general(a, b, (((1,), (1,)), ((), ())),
                           preferred_element_type=F32, precision=precision)


def _dot_tn(a, b, precision=None):
    return lax.dot_general(a, b, (((0,), (0,)), ((), ())),
                           preferred_element_type=F32, precision=precision)


def _silu(x):
    return x * (1.0 / (1.0 + jnp.exp(-x)))


def _sigmoid(x):
    return 1.0 / (1.0 + jnp.exp(-x))


def _shr(x, n):
    return lax.shift_right_logical(x, n)


def _proj_kernel(x_ref, nw_ref, w_ref, wsh_ref, wsl_ref, o_ref, os_ref, h_ref):
    j = pl.program_id(1)

    @pl.when(j == 0)
    def _():
        x = x_ref[...]
        ms = jnp.mean(x * x, axis=-1, keepdims=True)
        h = x * lax.rsqrt(ms + EPS) * nw_ref[...]
        h_hi = h.astype(BF16)
        h_lo = (h - h_hi.astype(F32)).astype(BF16)
        h_ref[...] = h_hi
        os_ref[...] = (_dot(h_hi, wsh_ref[...]) + _dot(h_lo, wsh_ref[...])
                       + _dot(h_hi, wsl_ref[...]))

    o_ref[...] = _dot(h_ref[...], w_ref[...]).astype(o_ref.dtype)


def _in_projection(x2, pre_norm_w, w_main, ws_hi, ws_lo, tm=2048):
    t = x2.shape[0]
    return pl.pallas_call(
        _proj_kernel,
        grid=(t // tm, N_MAIN_BLOCKS),
        in_specs=[
            pl.BlockSpec((tm, D_MODEL), lambda i, j: (i, 0)),
            pl.BlockSpec((1, D_MODEL), lambda i, j: (0, 0)),
            pl.BlockSpec((D_MODEL, D_MODEL), lambda i, j: (0, j)),
            pl.BlockSpec((D_MODEL, LANES), lambda i, j: (0, 0)),
            pl.BlockSpec((D_MODEL, LANES), lambda i, j: (0, 0)),
        ],
        out_specs=[
            pl.BlockSpec((tm, D_MODEL), lambda i, j: (i, j)),
            pl.BlockSpec((tm, LANES), lambda i, j: (i, 0)),
        ],
        out_shape=[
            jax.ShapeDtypeStruct((t, N_MAIN_BLOCKS * D_MODEL), BF16),
            jax.ShapeDtypeStruct((t, LANES), F32),
        ],
        scratch_shapes=[pltpu.VMEM((tm, D_MODEL), BF16)],
        compiler_params=pltpu.CompilerParams(
            dimension_semantics=("arbitrary", "arbitrary"),
            vmem_limit_bytes=VMEM_LIMIT),
        name="in_projection",
    )(x2, pre_norm_w, w_main, ws_hi, ws_lo)


def _rope(x, cos_t, sin_t):
    half = ROPE_DIM // 2
    lane = lax.broadcasted_iota(jnp.int32, x.shape, 1)
    partner = jnp.where(lane < half, pltpu.roll(x, LANES - half, 1), pltpu.roll(x, half, 1))
    return x * cos_t + partner * sin_t


def _split_bf16(x):
    hi = x.astype(BF16)
    return hi, (x - hi.astype(F32)).astype(BF16)


def _moba_kernel(q_ref, k_ref, v_ref, z_ref, cos_ref, sin_ref, o_ref, qa_s, ka_s, va_s):
    s_len = k_ref.shape[0]
    n_blocks = s_len // MOBA_BLOCK
    blk_shift = MOBA_BLOCK.bit_length() - 1
    scale = HEAD_DIM ** -0.5 * 1.4426950408889634
    cos_t = cos_ref[...]
    sin_t = sin_ref[...]
    row = lax.broadcasted_iota(jnp.int32, (s_len, LANES), 0)
    lane = lax.broadcasted_iota(jnp.int32, (s_len, LANES), 1)
    blk_of_row = _shr(row, blk_shift)

    kr = _rope(k_ref[...].astype(F32), cos_t, sin_t)
    ka_s[:, :HEAD_DIM] = kr.astype(BF16)
    ka_s[:, HEAD_DIM:] = jnp.where(blk_of_row == lane, 1.0, 0.0).astype(BF16)
    va_s[:, :HEAD_DIM] = v_ref[...]
    va_s[:, HEAD_DIM:] = jnp.ones((s_len, HEAD_DIM), BF16)
    km = jnp.mean(kr.reshape(n_blocks, MOBA_BLOCK, HEAD_DIM), axis=1)
    gate_rows = -(-n_blocks // 16) * 16
    if gate_rows > n_blocks:
        km = jnp.concatenate([km, jnp.zeros((gate_rows - n_blocks, HEAD_DIM), F32)], axis=0)

    qr = _rope(q_ref[...].astype(F32), cos_t, sin_t)
    qa_s[:, :HEAD_DIM] = (qr * scale).astype(BF16)
    q_hi, q_lo = _split_bf16(qr)
    km_hi, km_lo = _split_bf16(km)
    gate2 = _dot_nt(jnp.concatenate([km_hi, km_lo], axis=0), q_hi)
    gate_t = gate2[:gate_rows, :] + gate2[gate_rows:, :] + _dot_nt(km_hi, q_lo)
    blk_f = lax.broadcasted_iota(jnp.int32, (gate_rows, s_len), 0).astype(F32)
    own_f = _shr(lax.broadcasted_iota(jnp.int32, (gate_rows, s_len), 1), blk_shift).astype(F32)
    g = jnp.where(blk_f < own_f, gate_t, GATE_MASKED)
    sel = jnp.where(blk_f == own_f, 1.0, 0.0)
    for _ in range(MOBA_TOPK):
        m = jnp.max(g, axis=0, keepdims=True)
        cand = jnp.where((g == m) & (g > 0.5 * GATE_MASKED), blk_f, float(LANES))
        idx = jnp.min(cand, axis=0, keepdims=True)
        pick = blk_f == idx
        sel = jnp.where(pick, 1.0, sel)
        g = jnp.where(pick, GATE_MASKED, g)
    bias_t = jnp.concatenate([jnp.where(sel > 0.5, 0.0, NEG),
                              jnp.zeros((LANES - gate_rows, s_len), F32)], axis=0)
    for j in range(s_len // LANES):
        qa_s[j * LANES:(j + 1) * LANES, HEAD_DIM:] = (
            bias_t[:, j * LANES:(j + 1) * LANES].T.astype(BF16))

    rr = lax.broadcasted_iota(jnp.int32, (MOBA_BLOCK, MOBA_BLOCK), 0)
    cc = lax.broadcasted_iota(jnp.int32, (MOBA_BLOCK, MOBA_BLOCK), 1)
    causal = cc <= rr
    for t in range(n_blocks):
        r0 = t * MOBA_BLOCK
        n_keys = r0 + MOBA_BLOCK
        qa = qa_s[r0:n_keys, :]
        s_own = jnp.where(causal, _dot_nt(qa, ka_s[r0:n_keys, :]), NEG)
        if t > 0:
            s_all = jnp.concatenate([_dot_nt(qa, ka_s[0:r0, :]), s_own], axis=1)
        else:
            s_all = s_own
        m = jnp.max(s_all, axis=1, keepdims=True)
        p = jnp.exp2(s_all - m)
        acc = _dot(p.astype(BF16), va_s[0:n_keys, :])
        o = (acc[:, :HEAD_DIM] * (1.0 / acc[:, HEAD_DIM:])
             * _silu(z_ref[r0:n_keys, :].astype(F32)))
        o_ref[r0:n_keys, :] = o.astype(o_ref.dtype)


def _moba(proj3, cos_t, sin_t):
    b, s, _ = proj3.shape
    seq_spec = lambda col: pl.BlockSpec((None, s, HEAD_DIM), lambda bi, h: (bi, 0, col + h))
    table = pl.BlockSpec((s, LANES), lambda bi, h: (0, 0))
    return pl.pallas_call(
        _moba_kernel,
        grid=(b, N_HEADS),
        in_specs=[seq_spec(COL_QA), seq_spec(COL_KA), seq_spec(COL_VA), seq_spec(COL_ZA),
                  table, table],
        out_specs=pl.BlockSpec((None, s, HEAD_DIM), lambda bi, h: (bi, 0, h)),
        out_shape=jax.ShapeDtypeStruct((b, s, N_HEADS * HEAD_DIM), BF16),
        scratch_shapes=[
            pltpu.VMEM((s, 2 * HEAD_DIM), BF16),
            pltpu.VMEM((s, 2 * HEAD_DIM), BF16),
            pltpu.VMEM((s, 2 * HEAD_DIM), BF16),
        ],
        compiler_params=pltpu.CompilerParams(
            dimension_semantics=("arbitrary", "arbitrary"),
            vmem_limit_bytes=VMEM_LIMIT),
        name="moba_attention",
    )(proj3, proj3, proj3, proj3, cos_t, sin_t)


GDN_PAIR = 2 * GDN_CHUNK
GDN_PAD = 32
GDN_ROWS = 512
GDN_SCAN_GROUP = 4
GDN_UNROLL = 16
GDN_UNROLL_LEVELS = 32


def _gdn_kernel(q_ref, k_ref, v_ref, z_ref, sm_ref, cwq_ref, cwk_ref, cwv_ref,
                alane_ref, dlane_ref, nw_ref, o_ref,
                pad_s, ball_s, gall_s, gt_s, b_s, g_s,
                qb_s, qe_s, kb_s, kbb_s, kdt_s, rhs_s,
                m_s, x_s, qk_s, wu_s, c_s, bb_s, qp_s, op_s, st_s, gm_s, e_s):
    h = pl.program_id(1)
    s_len = q_ref.shape[0]
    c = GDN_CHUNK
    pp = GDN_PAIR
    n_chunks = s_len // c
    n_pairs = s_len // pp
    n_tiles = s_len // GDN_ROWS
    chunk_shift = c.bit_length() - 1

    def rows_of(i, size):
        return pl.ds(pl.multiple_of(i * size, size), size)

    @pl.when(h == 0)
    def _():
        pad_s[0:GDN_PAD, :] = jnp.zeros((GDN_PAD, LANES), F32)
        sm = sm_ref[...]
        ball_s[...] = _sigmoid(sm)
        xs = sm + dlane_ref[...]
        softplus = jnp.maximum(xs, 0.0) + jnp.log(1.0 + jnp.exp(-jnp.abs(xs)))
        pad_s[GDN_PAD:, :] = -jnp.exp(alane_ref[...]) * softplus
        pos = lax.broadcasted_iota(jnp.int32, (s_len, LANES), 0) & (c - 1)
        shift = 1
        while shift < c:
            cur = pad_s[GDN_PAD:, :]
            prev = pad_s[pl.ds(GDN_PAD - shift, s_len), :]
            pad_s[GDN_PAD:, :] = cur + jnp.where(pos >= shift, prev, 0.0)
            shift *= 2
        gall_s[...] = pad_s[GDN_PAD:, :]

        def tr(j, carry):
            gt_s[j] = gall_s[rows_of(j, pp), :].T
            return carry
        lax.fori_loop(0, n_pairs, tr, 0)

    def conv_silu_tile(i, cw):
        base = pl.multiple_of(i * GDN_ROWS, GDN_ROWS) + GDN_PAD
        y = pad_s[pl.ds(base, GDN_ROWS), :] * cw[CONV_WIDTH - 1:CONV_WIDTH, :]
        for back in range(1, CONV_WIDTH):
            y = y + (pad_s[pl.ds(base - back, GDN_ROWS), :]
                     * cw[CONV_WIDTH - 1 - back:CONV_WIDTH - back, :])
        return _silu(y)

    def l2n(x):
        return x * lax.rsqrt(jnp.sum(x * x, axis=-1, keepdims=True) + EPS)

    lane_t = lax.broadcasted_iota(jnp.int32, (GDN_ROWS, LANES), 1)

    def head_column(all_s, r, lane_idx):
        col = jnp.sum(jnp.where(lane_t == lane_idx, all_s[r, :], 0.0), axis=1, keepdims=True)
        return jnp.broadcast_to(col, (GDN_ROWS, LANES))

    pad_s[GDN_PAD:, :] = v_ref[...].astype(F32)
    cwv = cwv_ref[...]

    def v_tile(i, carry):
        r = rows_of(i, GDN_ROWS)
        bt = head_column(ball_s, r, h + LANE_BETA)
        b_s[r, :] = bt
        rhs_s[r, HEAD_DIM:2 * HEAD_DIM] = (conv_silu_tile(i, cwv) * bt).astype(BF16)
        return carry
    lax.fori_loop(0, n_tiles, v_tile, 0)

    pad_s[GDN_PAD:, :] = k_ref[...].astype(F32)
    cwk = cwk_ref[...]

    def k_tile(i, carry):
        r = rows_of(i, GDN_ROWS)
        gc = head_column(gall_s, r, h + LANE_DECAY)
        g_s[r, :] = gc
        kn = l2n(conv_silu_tile(i, cwk))
        kbeta = kn * b_s[r, :]
        kb_s[r, :] = kn.astype(BF16)
        kbb_s[r, :] = kbeta.astype(BF16)
        rhs_s[r, 0:HEAD_DIM] = (kbeta * jnp.exp(gc)).astype(BF16)
        g3 = gc.reshape(GDN_ROWS // c, c, LANES)
        kd = (kn.reshape(GDN_ROWS // c, c, LANES) * jnp.exp(g3[:, c - 1:c, :] - g3)
              ).reshape(GDN_ROWS, LANES)
        for p in range(GDN_ROWS // pp):
            kdt_s[i * (GDN_ROWS // pp) + p] = kd[p * pp:(p + 1) * pp, :].T.astype(BF16)
        return carry
    lax.fori_loop(0, n_tiles, k_tile, 0)

    pad_s[GDN_PAD:, :] = q_ref[...].astype(F32)
    cwq = cwq_ref[...]

    def q_tile(i, carry):
        r = rows_of(i, GDN_ROWS)
        y = conv_silu_tile(i, cwq)
        qn = y * (lax.rsqrt(jnp.sum(y * y, axis=-1, keepdims=True) + EPS) * (HEAD_DIM ** -0.5))
        qb_s[r, :] = qn.astype(BF16)
        qe_s[r, :] = qn * jnp.exp(g_s[r, :])
        return carry
    lax.fori_loop(0, n_tiles, q_tile, 0)

    ri = lax.broadcasted_iota(jnp.int32, (pp, pp), 0)
    ci = lax.broadcasted_iota(jnp.int32, (pp, pp), 1)
    same_chunk = _shr(ri, chunk_shift) == _shr(ci, chunk_shift)
    causal = same_chunk & (ci <= ri)
    strict_f = jnp.where(same_chunk & (ci < ri), 1.0, 0.0)
    eye = jnp.where(ci == ri, 1.0, 0.0)
    level_masks = []
    size = 1
    while size < c:
        sh = size.bit_length() - 1
        same_big = _shr(ri, sh + 1) == _shr(ci, sh + 1)
        diff_small = _shr(ri, sh) != _shr(ci, sh)
        level_masks.append(jnp.where(same_big & diff_small & (ci < ri), 1.0, 0.0))
        size *= 2

    def gram(j, carry):
        r = rows_of(j, pp)
        g_col = g_s[r, :]
        g_row = gt_s[j, pl.ds(h + LANE_DECAY, 1), :]
        diff = g_col - g_row
        decay = jnp.where(causal, jnp.exp(jnp.where(causal, diff, 0.0)), 0.0)
        kk = kb_s[r, :]
        m = _dot_nt(kbb_s[r, :], kk) * decay * strict_f
        qk_s[r, :] = (_dot_nt(qb_s[r, :], kk) * decay).astype(BF16)
        m_s[r, :] = m
        x_s[r, :] = eye - m * level_masks[0]
        return carry
    lax.fori_loop(0, n_pairs, gram, 0, unroll=min(n_pairs, GDN_UNROLL))

    size = 1
    for lm in level_masks[1:]:
        size *= 2
        if size % 8 == 0:
            def level(j, carry, lm=lm, size=size):
                base = pl.multiple_of(j * pp, pp)
                r = pl.ds(base, pp)
                odd_rows = [pl.ds(base + (2 * b + 1) * size, size) for b in range(pp // (2 * size))]
                xo = jnp.concatenate([x_s[rr, :] for rr in odd_rows], axis=0)
                y = _dot(xo.astype(BF16), (m_s[r, :] * lm).astype(BF16))
                new = xo - _dot(y.astype(BF16), x_s[r, :].astype(BF16))
                for b, rr in enumerate(odd_rows):
                    x_s[rr, :] = new[b * size:(b + 1) * size, :]
                return carry
        else:
            def level(j, carry, lm=lm):
                r = rows_of(j, pp)
                x = x_s[r, :]
                xb = x.astype(BF16)
                y = _dot(xb, (m_s[r, :] * lm).astype(BF16))
                x_s[r, :] = x - _dot(y.astype(BF16), xb)
                return carry
        lax.fori_loop(0, n_pairs, level, 0, unroll=min(n_pairs, GDN_UNROLL_LEVELS))

    def solve(j, carry):
        r = rows_of(j, pp)
        wu_s[r, :] = _dot(x_s[r, :].astype(BF16), rhs_s[r, :]).astype(BF16)
        return carry
    lax.fori_loop(0, n_pairs, solve, 0, unroll=min(n_pairs, GDN_UNROLL_LEVELS))

    def maps(j, carry):
        r = rows_of(j, pp)
        wu = wu_s[r, :]
        qo = _dot(qk_s[r, :], wu)
        qp_s[r, :] = (qe_s[r, :] - qo[:, 0:HEAD_DIM]).astype(BF16)
        op_s[r, :] = qo[:, HEAD_DIM:2 * HEAD_DIM]
        kdt = kdt_s[j]
        for half in range(2):
            cb = _dot(kdt[:, half * c:(half + 1) * c], wu[half * c:(half + 1) * c, :])
            c_s[2 * j + half] = (-cb[:, 0:HEAD_DIM]).astype(BF16)
            bb_s[2 * j + half] = cb[:, HEAD_DIM:2 * HEAD_DIM]
        return carry
    lax.fori_loop(0, n_pairs, maps, 0, unroll=min(n_pairs, GDN_UNROLL))

    grp = GDN_SCAN_GROUP
    n_groups = n_chunks // grp

    def glast_row(n):
        return g_s[pl.ds(n * c + (c - 1), 1), :]

    def first(gi, carry):
        n0 = gi * grp
        gm_s[pl.ds(n0, 1), :] = glast_row(n0)
        return carry
    lax.fori_loop(0, n_groups, first, 0, unroll=min(n_groups, GDN_UNROLL))

    for jj in range(1, grp):
        def compose(gi, carry, jj=jj):
            n = gi * grp + jj
            gsum = gm_s[pl.ds(n - 1, 1), :]
            glj = glast_row(n)
            n_mat = c_s[n]
            p_prev = c_s[n - 1]
            q_prev = bb_s[n - 1]
            res = _dot(n_mat, jnp.concatenate([p_prev, q_prev.astype(BF16)], axis=1))
            p_new = (jnp.exp(glj) * p_prev.astype(F32) + jnp.exp(gsum) * n_mat.astype(F32)
                     + res[:, 0:HEAD_DIM])
            c_s[n] = p_new.astype(BF16)
            bb_s[n] = jnp.exp(glj) * q_prev + res[:, HEAD_DIM:2 * HEAD_DIM] + bb_s[n]
            gm_s[pl.ds(n, 1), :] = gsum + glj
            return carry
        lax.fori_loop(0, n_groups, compose, 0, unroll=min(n_groups, GDN_UNROLL))

    gm_s[...] = jnp.exp(gm_s[...])

    def chain(gi, e):
        eb = e.astype(BF16)
        e_s[gi] = e
        st_s[gi * grp] = eb
        nl = gi * grp + (grp - 1)
        return e * gm_s[pl.ds(nl, 1), :] + _dot(c_s[nl], eb) + bb_s[nl]
    lax.fori_loop(0, n_groups, chain, jnp.zeros((HEAD_DIM, HEAD_DIM), F32))

    def fill(gi, carry):
        e = e_s[gi]
        eb = st_s[gi * grp]
        for jj in range(1, grp):
            n = gi * grp + jj
            st_s[n] = (e * gm_s[pl.ds(n - 1, 1), :] + _dot(c_s[n - 1], eb) + bb_s[n - 1]).astype(BF16)
        return carry
    lax.fori_loop(0, n_groups, fill, 0, unroll=min(n_groups, GDN_UNROLL))

    nw = nw_ref[...]

    def emit(n, carry):
        r = rows_of(n, c)
        o = _dot(qp_s[r, :], st_s[n]) + op_s[r, :]
        o = o * lax.rsqrt(jnp.mean(o * o, axis=-1, keepdims=True) + EPS) * nw
        o_ref[r, :] = (o * _silu(z_ref[r, :].astype(F32))).astype(o_ref.dtype)
        return carry
    lax.fori_loop(0, n_chunks, emit, 0, unroll=min(n_chunks, GDN_UNROLL))


def _gdn(proj3, small3, conv_w, alane, dlane, gdn_norm_w):
    b, s, _ = proj3.shape
    c = GDN_CHUNK
    seq_spec = lambda col: pl.BlockSpec((None, s, HEAD_DIM), lambda bi, h: (bi, 0, col + h))
    cw_spec = lambda off: pl.BlockSpec((CONV_WIDTH, HEAD_DIM), lambda bi, h: (0, off + h))
    row_spec = pl.BlockSpec((1, LANES), lambda bi, h: (0, 0))
    seq = lambda dt, w=HEAD_DIM: pltpu.VMEM((s, w), dt)
    return pl.pallas_call(
        _gdn_kernel,
        grid=(b, N_HEADS),
        in_specs=[
            seq_spec(COL_QB), seq_spec(COL_KB), seq_spec(COL_VB), seq_spec(COL_ZB),
            pl.BlockSpec((None, s, LANES), lambda bi, h: (bi, 0, 0)),
            cw_spec(0), cw_spec(N_HEADS), cw_spec(2 * N_HEADS),
            row_spec, row_spec, row_spec,
        ],
        out_specs=pl.BlockSpec((None, s, HEAD_DIM), lambda bi, h: (bi, 0, h)),
        out_shape=jax.ShapeDtypeStruct((b, s, N_HEADS * HEAD_DIM), BF16),
        scratch_shapes=[
            pltpu.VMEM((s + GDN_PAD, LANES), F32),
            seq(F32), seq(F32),
            pltpu.VMEM((s // GDN_PAIR, LANES, GDN_PAIR), F32),
            seq(F32), seq(F32),
            seq(BF16), seq(F32),
            seq(BF16), seq(BF16),
            pltpu.VMEM((s // GDN_PAIR, HEAD_DIM, GDN_PAIR), BF16),
            seq(BF16, 2 * HEAD_DIM),
            seq(F32), seq(F32), seq(BF16),
            seq(BF16, 2 * HEAD_DIM),
            pltpu.VMEM((s // c, HEAD_DIM, HEAD_DIM), BF16),
            pltpu.VMEM((s // c, HEAD_DIM, HEAD_DIM), F32),
            seq(BF16), seq(F32),
            pltpu.VMEM((s // c, HEAD_DIM, HEAD_DIM), BF16),
            pltpu.VMEM((s // c, LANES), F32),
            pltpu.VMEM((s // c // GDN_SCAN_GROUP, HEAD_DIM, HEAD_DIM), F32),
        ],
        compiler_params=pltpu.CompilerParams(
            dimension_semantics=("arbitrary", "arbitrary"),
            vmem_limit_bytes=VMEM_LIMIT),
        name="gated_deltanet",
    )(proj3, proj3, proj3, proj3, small3, conv_w, conv_w, conv_w, alane, dlane, gdn_norm_w)


def _out_kernel(x_ref, a_ref, b_ref, ga_ref, gb_ref, wa_ref, wb_ref, wo_ref, nw_ref, o_ref):
    y_a = _dot(a_ref[...], wa_ref[...])
    y_b = _dot(b_ref[...], wb_ref[...])
    merged = (_sigmoid(ga_ref[...].astype(F32)) * y_a
              + _sigmoid(gb_ref[...].astype(F32)) * y_b)
    out = _dot(merged.astype(BF16), wo_ref[...])
    ms = jnp.mean(out * out, axis=-1, keepdims=True)
    o_ref[...] = x_ref[...] + out * lax.rsqrt(ms + EPS) * nw_ref[...]


def _out_projection(x2, a2, b2, proj2, wa, wb, wo, post_norm_w, tm=512):
    t = x2.shape[0]
    tok = lambda: pl.BlockSpec((tm, D_MODEL), lambda i: (i, 0))
    full = lambda: pl.BlockSpec((D_MODEL, D_MODEL), lambda i: (0, 0))
    return pl.pallas_call(
        _out_kernel,
        grid=(t // tm,),
        in_specs=[
            tok(), tok(), tok(),
            pl.BlockSpec((tm, D_MODEL), lambda i: (i, BLK_GATE_A)),
            pl.BlockSpec((tm, D_MODEL), lambda i: (i, BLK_GATE_B)),
            full(), full(), full(),
            pl.BlockSpec((1, D_MODEL), lambda i: (0, 0)),
        ],
        out_specs=tok(),
        out_shape=jax.ShapeDtypeStruct((t, D_MODEL), F32),
        compiler_params=pltpu.CompilerParams(
            dimension_semantics=("arbitrary",),
            vmem_limit_bytes=VMEM_LIMIT),
        name="out_projection",
    )(x2, a2, b2, proj2, proj2, wa, wb, wo, post_norm_w)


def _rope_tables(s):
    half = ROPE_DIM // 2
    inv_freq = jnp.power(ROPE_THETA, -jnp.arange(half, dtype=F32) * (2.0 / ROPE_DIM))
    ang = jnp.arange(s, dtype=F32)[:, None] * inv_freq[None, :]
    cos, sin = jnp.cos(ang), jnp.sin(ang)
    ones = jnp.ones((s, HEAD_DIM - ROPE_DIM), F32)
    zeros = jnp.zeros((s, HEAD_DIM - ROPE_DIM), F32)
    cos_t = jnp.concatenate([cos, cos, ones], axis=1)
    sin_t = jnp.concatenate([-sin, sin, zeros], axis=1)
    return cos_t, sin_t


def _head_lanes(v):
    return jnp.pad(v.astype(F32), (LANE_DECAY, LANES - LANE_DECAY - N_HEADS)).reshape(1, LANES)


def _layer(x, pre_norm_w, w_in, conv_w, a_log, dt_bias, gdn_norm_w,
           w_branch_a, w_branch_b, w_out, post_norm_w):
    b, s, d = x.shape
    x2 = x.reshape(b * s, d)
    n_main = 8 * D_MODEL
    w_main = jnp.concatenate([w_in[:, :n_main], w_in[:, n_main + 2 * N_HEADS:]], axis=1).astype(BF16)
    w_small = jnp.pad(w_in[:, n_main:n_main + 2 * N_HEADS], ((0, 0), (0, LANES - 2 * N_HEADS)))
    ws_hi = w_small.astype(BF16)
    ws_lo = (w_small - ws_hi.astype(F32)).astype(BF16)

    proj, small = _in_projection(x2, pre_norm_w.reshape(1, d), w_main, ws_hi, ws_lo)
    proj3 = proj.reshape(b, s, N_MAIN_BLOCKS * D_MODEL)
    small3 = small.reshape(b, s, LANES)

    cos_t, sin_t = _rope_tables(s)
    act_a = _moba(proj3, cos_t, sin_t)
    act_b = _gdn(proj3, small3, conv_w, _head_lanes(a_log), _head_lanes(dt_bias),
                 gdn_norm_w.reshape(1, HEAD_DIM))

    out = _out_projection(x2, act_a.reshape(b * s, d), act_b.reshape(b * s, d), proj,
                          w_branch_a.astype(BF16), w_branch_b.astype(BF16), w_out.astype(BF16),
                          post_norm_w.reshape(1, d))
    return out.reshape(b, s, d)


def kernel(x, pre_norm_w, w_in, conv_w, a_log, dt_bias, gdn_norm_w,
           w_branch_a, w_branch_b, w_out, post_norm_w):
    for layer in range(pre_norm_w.shape[0]):
        x = _layer(x, pre_norm_w[layer], w_in[layer], conv_w[layer], a_log[layer],
                   dt_bias[layer], gdn_norm_w[layer], w_branch_a[layer],
                   w_branch_b[layer], w_out[layer], post_norm_w[layer])
    return x
```

```python
import jax
import jax.numpy as jnp
import numpy as np
from jax import lax
from jax.experimental import pallas as pl
from jax.experimental.pallas import tpu as pltpu

F32 = jnp.float32
BF16 = jnp.bfloat16

D_MODEL = 1024
N_HEADS = 8
HEAD_DIM = 128
MOBA_BLOCK = 256
MOBA_TOPK = 3
ROPE_THETA = 500000.0
ROPE_DIM = HEAD_DIM // 4
CONV_WIDTH = 4
GDN_CHUNK = 64
EPS = 1e-6
NEG = -1e30
GATE_MASKED = -3.0e38
LANES = 128
N_MAIN_BLOCKS = 10

COL_QA, COL_KA, COL_VA, COL_ZA = 0, 8, 16, 24
COL_QB, COL_KB, COL_VB, COL_ZB = 32, 40, 48, 56
BLK_GATE_A, BLK_GATE_B = 8, 9
LANE_BETA, LANE_DECAY = 0, N_HEADS

VMEM_LIMIT = 58 * 1024 * 1024
HIGHEST = lax.Precision.HIGHEST


def _dot(a, b, precision=None):
    return jnp.dot(a, b, preferred_element_type=F32, precision=precision)


def _dot_nt(a, b, precision=None):
    return lax.dot_general(a, b, (((1,), (1,)), ((), ())),
                           preferred_element_type=F32, precision=precision)


def _dot_tn(a, b, precision=None):
    return lax.dot_general(a, b, (((0,), (0,)), ((), ())),
                           preferred_element_type=F32, precision=precision)


def _silu(x):
    return x * (1.0 / (1.0 + jnp.exp(-x)))


def _sigmoid(x):
    return 1.0 / (1.0 + jnp.exp(-x))


def _shr(x, n):
    return lax.shift_right_logical(x, n)


def _proj_kernel(x_ref, nw_ref, w_ref, ws_ref, o_ref, os_ref, h_ref, inv_ref):
    j = pl.program_id(1)

    @pl.when(j == 0)
    def _():
        x = x_ref[...]
        inv = lax.rsqrt(jnp.mean(x * x, axis=-1, keepdims=True) + EPS)
        inv_ref[...] = inv
        hb = (x * nw_ref[...]).astype(BF16)
        h_ref[...] = hb
        os_ref[...] = _dot(hb, ws_ref[...]) * inv

    o_ref[...] = (_dot(h_ref[...], w_ref[...]) * inv_ref[...]).astype(o_ref.dtype)


def _in_projection(x2, pre_norm_w, w_main, w_small, tm=2048, tn=1024):
    t = x2.shape[0]
    n_cols = w_main.shape[1]
    return pl.pallas_call(
        _proj_kernel,
        grid=(t // tm, n_cols // tn),
        in_specs=[
            pl.BlockSpec((tm, D_MODEL), lambda i, j: (i, 0)),
            pl.BlockSpec((1, D_MODEL), lambda i, j: (0, 0)),
            pl.BlockSpec((D_MODEL, tn), lambda i, j: (0, j)),
            pl.BlockSpec((D_MODEL, LANES), lambda i, j: (0, 0)),
        ],
        out_specs=[
            pl.BlockSpec((tm, tn), lambda i, j: (i, j)),
            pl.BlockSpec((tm, LANES), lambda i, j: (i, 0)),
        ],
        out_shape=[
            jax.ShapeDtypeStruct((t, n_cols), BF16),
            jax.ShapeDtypeStruct((t, LANES), F32),
        ],
        scratch_shapes=[pltpu.VMEM((tm, D_MODEL), BF16), pltpu.VMEM((tm, 1), F32)],
        compiler_params=pltpu.CompilerParams(
            dimension_semantics=("arbitrary", "arbitrary"),
            vmem_limit_bytes=VMEM_LIMIT),
        name="in_projection",
    )(x2, pre_norm_w, w_main, w_small)


def _rope(x, cos_t, sin_t):
    half = ROPE_DIM // 2
    lane = lax.broadcasted_iota(jnp.int32, x.shape, 1)
    partner = jnp.where(lane < half, pltpu.roll(x, LANES - half, 1), pltpu.roll(x, half, 1))
    return x * cos_t + partner * sin_t


def _split_bf16(x):
    hi = x.astype(BF16)
    return hi, (x - hi.astype(F32)).astype(BF16)


def _moba_kernel(q_ref, k_ref, v_ref, z_ref, cos_ref, sin_ref, o_ref, qa_s, ka_s, va_s):
    s_len = k_ref.shape[0]
    n_blocks = s_len // MOBA_BLOCK
    blk_shift = MOBA_BLOCK.bit_length() - 1
    scale = HEAD_DIM ** -0.5 * 1.4426950408889634
    cos_t = cos_ref[...]
    sin_t = sin_ref[...]
    row = lax.broadcasted_iota(jnp.int32, (s_len, LANES), 0)
    lane = lax.broadcasted_iota(jnp.int32, (s_len, LANES), 1)
    blk_of_row = _shr(row, blk_shift)

    kr = _rope(k_ref[...].astype(F32), cos_t, sin_t)
    ka_s[:, :HEAD_DIM] = kr.astype(BF16)
    ka_s[:, HEAD_DIM:] = jnp.where(blk_of_row == lane, 1.0, 0.0).astype(BF16)
    va_s[:, :HEAD_DIM] = v_ref[...]
    va_s[:, HEAD_DIM:] = jnp.ones((s_len, HEAD_DIM), BF16)
    km = jnp.mean(kr.reshape(n_blocks, MOBA_BLOCK, HEAD_DIM), axis=1)
    gate_rows = -(-n_blocks // 16) * 16
    if gate_rows > n_blocks:
        km = jnp.concatenate([km, jnp.zeros((gate_rows - n_blocks, HEAD_DIM), F32)], axis=0)

    qr = _rope(q_ref[...].astype(F32), cos_t, sin_t)
    qa_s[:, :HEAD_DIM] = (qr * scale).astype(BF16)
    q_hi, q_lo = _split_bf16(qr)
    km_hi, km_lo = _split_bf16(km)
    gate2 = _dot_nt(jnp.concatenate([km_hi, km_lo], axis=0), q_hi)
    gate_t = gate2[:gate_rows, :] + gate2[gate_rows:, :] + _dot_nt(km_hi, q_lo)
    blk_f = lax.broadcasted_iota(jnp.int32, (gate_rows, s_len), 0).astype(F32)
    own_f = _shr(lax.broadcasted_iota(jnp.int32, (gate_rows, s_len), 1), blk_shift).astype(F32)
    g = jnp.where(blk_f < own_f, gate_t, GATE_MASKED)
    sel = jnp.where(blk_f == own_f, 1.0, 0.0)
    for _ in range(MOBA_TOPK):
        m = jnp.max(g, axis=0, keepdims=True)
        cand = jnp.where((g == m) & (g > 0.5 * GATE_MASKED), blk_f, float(LANES))
        idx = jnp.min(cand, axis=0, keepdims=True)
        pick = blk_f == idx
        sel = jnp.where(pick, 1.0, sel)
        g = jnp.where(pick, GATE_MASKED, g)
    bias_t = jnp.concatenate([jnp.where(sel > 0.5, 0.0, NEG),
                              jnp.zeros((LANES - gate_rows, s_len), F32)], axis=0)
    for j in range(s_len // LANES):
        qa_s[j * LANES:(j + 1) * LANES, HEAD_DIM:] = (
            bias_t[:, j * LANES:(j + 1) * LANES].T.astype(BF16))

    rr = lax.broadcasted_iota(jnp.int32, (MOBA_BLOCK, MOBA_BLOCK), 0)
    cc = lax.broadcasted_iota(jnp.int32, (MOBA_BLOCK, MOBA_BLOCK), 1)
    causal = cc <= rr
    for t in range(n_blocks):
        r0 = t * MOBA_BLOCK
        n_keys = r0 + MOBA_BLOCK
        qa = qa_s[r0:n_keys, :]
        s_own = jnp.where(causal, _dot_nt(qa, ka_s[r0:n_keys, :]), NEG)
        if t > 0:
            s_all = jnp.concatenate([_dot_nt(qa, ka_s[0:r0, :]), s_own], axis=1)
        else:
            s_all = s_own
        m = jnp.max(s_all, axis=1, keepdims=True)
        p = jnp.exp2(s_all - m)
        acc = _dot(p.astype(BF16), va_s[0:n_keys, :])
        o = (acc[:, :HEAD_DIM] * (1.0 / acc[:, HEAD_DIM:])
             * _silu(z_ref[r0:n_keys, :].astype(F32)))
        o_ref[r0:n_keys, :] = o.astype(o_ref.dtype)


def _moba(proj3, cos_t, sin_t):
    b, s, _ = proj3.shape
    seq_spec = lambda col: pl.BlockSpec((None, s, HEAD_DIM), lambda bi, h: (bi, 0, col + h))
    table = pl.BlockSpec((s, LANES), lambda bi, h: (0, 0))
    return pl.pallas_call(
        _moba_kernel,
        grid=(b, N_HEADS),
        in_specs=[seq_spec(COL_QA), seq_spec(COL_KA), seq_spec(COL_VA), seq_spec(COL_ZA),
                  table, table],
        out_specs=pl.BlockSpec((None, s, HEAD_DIM), lambda bi, h: (bi, 0, h)),
        out_shape=jax.ShapeDtypeStruct((b, s, N_HEADS * HEAD_DIM), BF16),
        scratch_shapes=[
            pltpu.VMEM((s, 2 * HEAD_DIM), BF16),
            pltpu.VMEM((s, 2 * HEAD_DIM), BF16),
            pltpu.VMEM((s, 2 * HEAD_DIM), BF16),
        ],
        compiler_params=pltpu.CompilerParams(
            dimension_semantics=("arbitrary", "arbitrary"),
            vmem_limit_bytes=VMEM_LIMIT),
        name="moba_attention",
    )(proj3, proj3, proj3, proj3, cos_t, sin_t)


GDN_PAIR = 2 * GDN_CHUNK
GDN_PAD = 32
GDN_ROWS = 512
GDN_SCAN_GROUP = 4
GDN_UNROLL = 16
GDN_UNROLL_LEVELS = 32


def _gdn_kernel(q_ref, k_ref, v_ref, z_ref, sm_ref, cwq_ref, cwk_ref, cwv_ref,
                alane_ref, dlane_ref, nw_ref, o_ref,
                pad_s, ball_s, gall_s, gt_s, b_s, g_s,
                qb_s, qe_s, kb_s, kbb_s, kdt_s, rhs_s,
                m_s, x_s, qk_s, wu_s, c_s, bb_s, qp_s, op_s, st_s, gm_s, e_s):
    h = pl.program_id(1)
    s_len = q_ref.shape[0]
    c = GDN_CHUNK
    pp = GDN_PAIR
    n_chunks = s_len // c
    n_pairs = s_len // pp
    n_tiles = s_len // GDN_ROWS
    chunk_shift = c.bit_length() - 1

    def rows_of(i, size):
        return pl.ds(pl.multiple_of(i * size, size), size)

    @pl.when(h == 0)
    def _():
        pad_s[0:GDN_PAD, :] = jnp.zeros((GDN_PAD, LANES), F32)
        sm = sm_ref[...]
        ball_s[...] = _sigmoid(sm)
        xs = sm + dlane_ref[...]
        softplus = jnp.maximum(xs, 0.0) + jnp.log(1.0 + jnp.exp(-jnp.abs(xs)))
        pad_s[GDN_PAD:, :] = -jnp.exp(alane_ref[...]) * softplus
        pos = lax.broadcasted_iota(jnp.int32, (s_len, LANES), 0) & (c - 1)
        shift = 1
        while shift < c:
            cur = pad_s[GDN_PAD:, :]
            prev = pad_s[pl.ds(GDN_PAD - shift, s_len), :]
            pad_s[GDN_PAD:, :] = cur + jnp.where(pos >= shift, prev, 0.0)
            shift *= 2
        gall_s[...] = pad_s[GDN_PAD:, :]

        def tr(j, carry):
            gt_s[j] = gall_s[rows_of(j, pp), :].T
            return carry
        lax.fori_loop(0, n_pairs, tr, 0)

    def conv_silu_tile(i, cw):
        base = pl.multiple_of(i * GDN_ROWS, GDN_ROWS) + GDN_PAD
        y = pad_s[pl.ds(base, GDN_ROWS), :] * cw[CONV_WIDTH - 1:CONV_WIDTH, :]
        for back in range(1, CONV_WIDTH):
            y = y + (pad_s[pl.ds(base - back, GDN_ROWS), :]
                     * cw[CONV_WIDTH - 1 - back:CONV_WIDTH - back, :])
        return _silu(y)

    def l2n(x):
        return x * lax.rsqrt(jnp.sum(x * x, axis=-1, keepdims=True) + EPS)

    lane_t = lax.broadcasted_iota(jnp.int32, (GDN_ROWS, LANES), 1)

    def head_column(all_s, r, lane_idx):
        col = jnp.sum(jnp.where(lane_t == lane_idx, all_s[r, :], 0.0), axis=1, keepdims=True)
        return jnp.broadcast_to(col, (GDN_ROWS, LANES))

    pad_s[GDN_PAD:, :] = v_ref[...].astype(F32)
    cwv = cwv_ref[...]

    def v_tile(i, carry):
        r = rows_of(i, GDN_ROWS)
        bt = head_column(ball_s, r, h + LANE_BETA)
        b_s[r, :] = bt
        rhs_s[r, HEAD_DIM:2 * HEAD_DIM] = (conv_silu_tile(i, cwv) * bt).astype(BF16)
        return carry
    lax.fori_loop(0, n_tiles, v_tile, 0)

    pad_s[GDN_PAD:, :] = k_ref[...].astype(F32)
    cwk = cwk_ref[...]

    def k_tile(i, carry):
        r = rows_of(i, GDN_ROWS)
        gc = head_column(gall_s, r, h + LANE_DECAY)
        g_s[r, :] = gc
        kn = l2n(conv_silu_tile(i, cwk))
        kbeta = kn * b_s[r, :]
        kb_s[r, :] = kn.astype(BF16)
        kbb_s[r, :] = kbeta.astype(BF16)
        rhs_s[r, 0:HEAD_DIM] = (kbeta * jnp.exp(gc)).astype(BF16)
        g3 = gc.reshape(GDN_ROWS // c, c, LANES)
        kd = (kn.reshape(GDN_ROWS // c, c, LANES) * jnp.exp(g3[:, c - 1:c, :] - g3)
              ).reshape(GDN_ROWS, LANES)
        for p in range(GDN_ROWS // pp):
            kdt_s[i * (GDN_ROWS // pp) + p] = kd[p * pp:(p + 1) * pp, :].T.astype(BF16)
        return carry
    lax.fori_loop(0, n_tiles, k_tile, 0)

    pad_s[GDN_PAD:, :] = q_ref[...].astype(F32)
    cwq = cwq_ref[...]

    def q_tile(i, carry):
        r = rows_of(i, GDN_ROWS)
        y = conv_silu_tile(i, cwq)
        qn = y * (lax.rsqrt(jnp.sum(y * y, axis=-1, keepdims=True) + EPS) * (HEAD_DIM ** -0.5))
        qb_s[r, :] = qn.astype(BF16)
        qe_s[r, :] = qn * jnp.exp(g_s[r, :])
        return carry
    lax.fori_loop(0, n_tiles, q_tile, 0)

    ri = lax.broadcasted_iota(jnp.int32, (pp, pp), 0)
    ci = lax.broadcasted_iota(jnp.int32, (pp, pp), 1)
    same_chunk = _shr(ri, chunk_shift) == _shr(ci, chunk_shift)
    causal = same_chunk & (ci <= ri)
    strict_f = jnp.where(same_chunk & (ci < ri), 1.0, 0.0)
    eye = jnp.where(ci == ri, 1.0, 0.0)
    level_masks = []
    size = 1
    while size < c:
        sh = size.bit_length() - 1
        same_big = _shr(ri, sh + 1) == _shr(ci, sh + 1)
        diff_small = _shr(ri, sh) != _shr(ci, sh)
        level_masks.append(jnp.where(same_big & diff_small & (ci < ri), 1.0, 0.0))
        size *= 2

    def gram(j, carry):
        r = rows_of(j, pp)
        g_col = g_s[r, :]
        g_row = gt_s[j, pl.ds(h + LANE_DECAY, 1), :]
        diff = g_col - g_row
        decay = jnp.where(causal, jnp.exp(jnp.where(causal, diff, 0.0)), 0.0)
        kk = kb_s[r, :]
        m = _dot_nt(kbb_s[r, :], kk) * decay * strict_f
        qk_s[r, :] = (_dot_nt(qb_s[r, :], kk) * decay).astype(BF16)
        m_s[r, :] = m
        x_s[r, :] = eye - m * level_masks[0]
        return carry
    lax.fori_loop(0, n_pairs, gram, 0, unroll=min(n_pairs, GDN_UNROLL))

    size = 1
    for lm in level_masks[1:]:
        size *= 2
        if size % 8 == 0:
            def level(j, carry, lm=lm, size=size):
                base = pl.multiple_of(j * pp, pp)
                r = pl.ds(base, pp)
                odd_rows = [pl.ds(base + (2 * b + 1) * size, size) for b in range(pp // (2 * size))]
                xo = jnp.concatenate([x_s[rr, :] for rr in odd_rows], axis=0)
                y = _dot(xo.astype(BF16), (m_s[r, :] * lm).astype(BF16))
                new = xo - _dot(y.astype(BF16), x_s[r, :].astype(BF16))
                for b, rr in enumerate(odd_rows):
                    x_s[rr, :] = new[b * size:(b + 1) * size, :]
                return carry
        else:
            def level(j, carry, lm=lm):
                r = rows_of(j, pp)
                x = x_s[r, :]
                xb = x.astype(BF16)
                y = _dot(xb, (m_s[r, :] * lm).astype(BF16))
                x_s[r, :] = x - _dot(y.astype(BF16), xb)
                return carry
        lax.fori_loop(0, n_pairs, level, 0, unroll=min(n_pairs, GDN_UNROLL_LEVELS))

    def solve(j, carry):
        r = rows_of(j, pp)
        wu_s[r, :] = _dot(x_s[r, :].astype(BF16), rhs_s[r, :]).astype(BF16)
        return carry
    lax.fori_loop(0, n_pairs, solve, 0, unroll=min(n_pairs, GDN_UNROLL_LEVELS))

    def maps(j, carry):
        r = rows_of(j, pp)
        wu = wu_s[r, :]
        qo = _dot(qk_s[r, :], wu)
        qp_s[r, :] = (qe_s[r, :] - qo[:, 0:HEAD_DIM]).astype(BF16)
        op_s[r, :] = qo[:, HEAD_DIM:2 * HEAD_DIM]
        kdt = kdt_s[j]
        for half in range(2):
            cb = _dot(kdt[:, half * c:(half + 1) * c], wu[half * c:(half + 1) * c, :])
            c_s[2 * j + half] = (-cb[:, 0:HEAD_DIM]).astype(BF16)
            bb_s[2 * j + half] = cb[:, HEAD_DIM:2 * HEAD_DIM]
        return carry
    lax.fori_loop(0, n_pairs, maps, 0, unroll=min(n_pairs, GDN_UNROLL))

    grp = GDN_SCAN_GROUP
    n_groups = n_chunks // grp

    def glast_row(n):
        return g_s[pl.ds(n * c + (c - 1), 1), :]

    def first(gi, carry):
        n0 = gi * grp
        gm_s[pl.ds(n0, 1), :] = glast_row(n0)
        return carry
    lax.fori_loop(0, n_groups, first, 0, unroll=min(n_groups, GDN_UNROLL))

    for jj in range(1, grp):
        def compose(gi, carry, jj=jj):
            n = gi * grp + jj
            gsum = gm_s[pl.ds(n - 1, 1), :]
            glj = glast_row(n)
            n_mat = c_s[n]
            p_prev = c_s[n - 1]
            q_prev = bb_s[n - 1]
            res = _dot(n_mat, jnp.concatenate([p_prev, q_prev.astype(BF16)], axis=1))
            p_new = (jnp.exp(glj) * p_prev.astype(F32) + jnp.exp(gsum) * n_mat.astype(F32)
                     + res[:, 0:HEAD_DIM])
            c_s[n] = p_new.astype(BF16)
            bb_s[n] = jnp.exp(glj) * q_prev + res[:, HEAD_DIM:2 * HEAD_DIM] + bb_s[n]
            gm_s[pl.ds(n, 1), :] = gsum + glj
            return carry
        lax.fori_loop(0, n_groups, compose, 0, unroll=min(n_groups, GDN_UNROLL))

    gm_s[...] = jnp.exp(gm_s[...])

    def chain(gi, e):
        eb = e.astype(BF16)
        e_s[gi] = e
        st_s[gi * grp] = eb
        nl = gi * grp + (grp - 1)
        return e * gm_s[pl.ds(nl, 1), :] + _dot(c_s[nl], eb) + bb_s[nl]
    lax.fori_loop(0, n_groups, chain, jnp.zeros((HEAD_DIM, HEAD_DIM), F32))

    def fill(gi, carry):
        e = e_s[gi]
        eb = st_s[gi * grp]
        for jj in range(1, grp):
            n = gi * grp + jj
            st_s[n] = (e * gm_s[pl.ds(n - 1, 1), :] + _dot(c_s[n - 1], eb) + bb_s[n - 1]).astype(BF16)
        return carry
    lax.fori_loop(0, n_groups, fill, 0, unroll=min(n_groups, GDN_UNROLL))

    nw = nw_ref[...]

    def emit(n, carry):
        r = rows_of(n, c)
        o = _dot(qp_s[r, :], st_s[n]) + op_s[r, :]
        o = o * lax.rsqrt(jnp.mean(o * o, axis=-1, keepdims=True) + EPS) * nw
        o_ref[r, :] = (o * _silu(z_ref[r, :].astype(F32))).astype(o_ref.dtype)
        return carry
    lax.fori_loop(0, n_chunks, emit, 0, unroll=min(n_chunks, GDN_UNROLL))


def _gdn(proj3, small3, conv_w, alane, dlane, gdn_norm_w):
    b, s, _ = proj3.shape
    c = GDN_CHUNK
    seq_spec = lambda col: pl.BlockSpec((None, s, HEAD_DIM), lambda bi, h: (bi, 0, col + h))
    cw_spec = lambda off: pl.BlockSpec((CONV_WIDTH, HEAD_DIM), lambda bi, h: (0, off + h))
    row_spec = pl.BlockSpec((1, LANES), lambda bi, h: (0, 0))
    seq = lambda dt, w=HEAD_DIM: pltpu.VMEM((s, w), dt)
    return pl.pallas_call(
        _gdn_kernel,
        grid=(b, N_HEADS),
        in_specs=[
            seq_spec(COL_QB), seq_spec(COL_KB), seq_spec(COL_VB), seq_spec(COL_ZB),
            pl.BlockSpec((None, s, LANES), lambda bi, h: (bi, 0, 0)),
            cw_spec(0), cw_spec(N_HEADS), cw_spec(2 * N_HEADS),
            row_spec, row_spec, row_spec,
        ],
        out_specs=pl.BlockSpec((None, s, HEAD_DIM), lambda bi, h: (bi, 0, h)),
        out_shape=jax.ShapeDtypeStruct((b, s, N_HEADS * HEAD_DIM), BF16),
        scratch_shapes=[
            pltpu.VMEM((s + GDN_PAD, LANES), F32),
            seq(F32), seq(F32),
            pltpu.VMEM((s // GDN_PAIR, LANES, GDN_PAIR), F32),
            seq(F32), seq(F32),
            seq(BF16), seq(F32),
            seq(BF16), seq(BF16),
            pltpu.VMEM((s // GDN_PAIR, HEAD_DIM, GDN_PAIR), BF16),
            seq(BF16, 2 * HEAD_DIM),
            seq(F32), seq(F32), seq(BF16),
            seq(BF16, 2 * HEAD_DIM),
            pltpu.VMEM((s // c, HEAD_DIM, HEAD_DIM), BF16),
            pltpu.VMEM((s // c, HEAD_DIM, HEAD_DIM), F32),
            seq(BF16), seq(F32),
            pltpu.VMEM((s // c, HEAD_DIM, HEAD_DIM), BF16),
            pltpu.VMEM((s // c, LANES), F32),
            pltpu.VMEM((s // c // GDN_SCAN_GROUP, HEAD_DIM, HEAD_DIM), F32),
        ],
        compiler_params=pltpu.CompilerParams(
            dimension_semantics=("arbitrary", "arbitrary"),
            vmem_limit_bytes=VMEM_LIMIT),
        name="gated_deltanet",
    )(proj3, proj3, proj3, proj3, small3, conv_w, conv_w, conv_w, alane, dlane, gdn_norm_w)


def _out_kernel(x_ref, a_ref, b_ref, ga_ref, gb_ref, wa_ref, wb_ref, wo_ref, nw_ref, o_ref):
    y_a = _dot(a_ref[...], wa_ref[...])
    y_b = _dot(b_ref[...], wb_ref[...])
    merged = (_sigmoid(ga_ref[...].astype(F32)) * y_a
              + _sigmoid(gb_ref[...].astype(F32)) * y_b)
    out = _dot(merged.astype(BF16), wo_ref[...])
    ms = jnp.mean(out * out, axis=-1, keepdims=True)
    o_ref[...] = x_ref[...] + out * lax.rsqrt(ms + EPS) * nw_ref[...]


def _out_projection(x2, a2, b2, proj2, wa, wb, wo, post_norm_w, tm=512):
    t = x2.shape[0]
    tok = lambda: pl.BlockSpec((tm, D_MODEL), lambda i: (i, 0))
    full = lambda: pl.BlockSpec((D_MODEL, D_MODEL), lambda i: (0, 0))
    return pl.pallas_call(
        _out_kernel,
        grid=(t // tm,),
        in_specs=[
            tok(), tok(), tok(),
            pl.BlockSpec((tm, D_MODEL), lambda i: (i, BLK_GATE_A)),
            pl.BlockSpec((tm, D_MODEL), lambda i: (i, BLK_GATE_B)),
            full(), full(), full(),
            pl.BlockSpec((1, D_MODEL), lambda i: (0, 0)),
        ],
        out_specs=tok(),
        out_shape=jax.ShapeDtypeStruct((t, D_MODEL), F32),
        compiler_params=pltpu.CompilerParams(
            dimension_semantics=("arbitrary",),
            vmem_limit_bytes=VMEM_LIMIT),
        name="out_projection",
    )(x2, a2, b2, proj2, proj2, wa, wb, wo, post_norm_w)


def _rope_tables(s):
    half = ROPE_DIM // 2
    inv_freq = np.power(ROPE_THETA, -np.arange(half, dtype=np.float64) * (2.0 / ROPE_DIM))
    ang = np.arange(s, dtype=np.float64)[:, None] * inv_freq[None, :]
    cos, sin = np.cos(ang), np.sin(ang)
    ones = np.ones((s, HEAD_DIM - ROPE_DIM))
    zeros = np.zeros((s, HEAD_DIM - ROPE_DIM))
    cos_t = np.concatenate([cos, cos, ones], axis=1).astype(np.float32)
    sin_t = np.concatenate([-sin, sin, zeros], axis=1).astype(np.float32)
    return jnp.asarray(cos_t), jnp.asarray(sin_t)


def _head_lanes(v):
    return jnp.pad(v.astype(F32), (LANE_DECAY, LANES - LANE_DECAY - N_HEADS)).reshape(1, LANES)


def _layer(x, pre_norm_w, w_in, conv_w, a_log, dt_bias, gdn_norm_w,
           w_branch_a, w_branch_b, w_out, post_norm_w):
    b, s, d = x.shape
    x2 = x.reshape(b * s, d)
    n_main = 8 * D_MODEL
    w_main = jnp.concatenate([w_in[:, :n_main], w_in[:, n_main + 2 * N_HEADS:]], axis=1).astype(BF16)
    w_small = jnp.pad(w_in[:, n_main:n_main + 2 * N_HEADS],
                      ((0, 0), (0, LANES - 2 * N_HEADS))).astype(BF16)

    proj, small = _in_projection(x2, pre_norm_w.reshape(1, d), w_main, w_small)
    proj3 = proj.reshape(b, s, N_MAIN_BLOCKS * D_MODEL)
    small3 = small.reshape(b, s, LANES)

    cos_t, sin_t = _rope_tables(s)
    act_a = _moba(proj3, cos_t, sin_t)
    act_b = _gdn(proj3, small3, conv_w, _head_lanes(a_log), _head_lanes(dt_bias),
                 gdn_norm_w.reshape(1, HEAD_DIM))

    out = _out_projection(x2, act_a.reshape(b * s, d), act_b.reshape(b * s, d), proj,
                          w_branch_a.astype(BF16), w_branch_b.astype(BF16), w_out.astype(BF16),
                          post_norm_w.reshape(1, d))
    return out.reshape(b, s, d)


def kernel(x, pre_norm_w, w_in, conv_w, a_log, dt_bias, gdn_norm_w,
           w_branch_a, w_branch_b, w_out, post_norm_w):
    for layer in range(pre_norm_w.shape[0]):
        x = _layer(x, pre_norm_w[layer], w_in[layer], conv_w[layer], a_log[layer],
                   dt_bias[layer], gdn_norm_w[layer], w_branch_a[layer],
                   w_branch_b[layer], w_out[layer], post_norm_w[layer])
    return x
```

```python
import jax
import jax.numpy as jnp
import numpy as np
from jax import lax
from jax.experimental import pallas as pl
from jax.experimental.pallas import tpu as pltpu

F32 = jnp.float32
BF16 = jnp.bfloat16

D_MODEL = 1024
N_HEADS = 8
HEAD_DIM = 128
MOBA_BLOCK = 256
MOBA_TOPK = 3
ROPE_THETA = 500000.0
ROPE_DIM = HEAD_DIM // 4
CONV_WIDTH = 4
GDN_CHUNK = 64
EPS = 1e-6
NEG = -1e30
GATE_MASKED = -3.0e38
LANES = 128
N_MAIN_BLOCKS = 10

COL_QA, COL_KA, COL_VA, COL_ZA = 0, 8, 16, 24
COL_QB, COL_KB, COL_VB, COL_ZB = 32, 40, 48, 56
BLK_GATE_A, BLK_GATE_B = 8, 9
LANE_BETA, LANE_DECAY = 0, N_HEADS

VMEM_LIMIT = 58 * 1024 * 1024
HIGHEST = lax.Precision.HIGHEST


def _dot(a, b, precision=None):
    return jnp.dot(a, b, preferred_element_type=F32, precision=precision)


def _dot_nt(a, b, precision=None):
    return lax.dot_general(a, b, (((1,), (1,)), ((), ())),
                           preferred_element_type=F32, precision=precision)


def _dot_tn(a, b, precision=None):
    return lax.dot_general(a, b, (((0,), (0,)), ((), ())),
                           preferred_element_type=F32, precision=precision)


def _silu(x):
    return x * (1.0 / (1.0 + jnp.exp(-x)))


def _sigmoid(x):
    return 1.0 / (1.0 + jnp.exp(-x))


def _shr(x, n):
    return lax.shift_right_logical(x, n)


def _proj_kernel(x_ref, nw_ref, w_ref, ws_ref, o_ref, os_ref, h_ref, inv_ref):
    j = pl.program_id(1)

    @pl.when(j == 0)
    def _():
        x = x_ref[...]
        inv = lax.rsqrt(jnp.mean(x * x, axis=-1, keepdims=True) + EPS)
        inv_ref[...] = inv
        hb = (x * nw_ref[...]).astype(BF16)
        h_ref[...] = hb
        os_ref[...] = _dot(hb, ws_ref[...]) * inv

    o_ref[...] = (_dot(h_ref[...], w_ref[...]) * inv_ref[...]).astype(o_ref.dtype)


def _in_projection(x2, pre_norm_w, w_main, w_small, tm=2048, tn=1024):
    t = x2.shape[0]
    n_cols = w_main.shape[1]
    return pl.pallas_call(
        _proj_kernel,
        grid=(t // tm, n_cols // tn),
        in_specs=[
            pl.BlockSpec((tm, D_MODEL), lambda i, j: (i, 0)),
            pl.BlockSpec((1, D_MODEL), lambda i, j: (0, 0)),
            pl.BlockSpec((D_MODEL, tn), lambda i, j: (0, j)),
            pl.BlockSpec((D_MODEL, LANES), lambda i, j: (0, 0)),
        ],
        out_specs=[
            pl.BlockSpec((tm, tn), lambda i, j: (i, j)),
            pl.BlockSpec((tm, LANES), lambda i, j: (i, 0)),
        ],
        out_shape=[
            jax.ShapeDtypeStruct((t, n_cols), BF16),
            jax.ShapeDtypeStruct((t, LANES), F32),
        ],
        scratch_shapes=[pltpu.VMEM((tm, D_MODEL), BF16), pltpu.VMEM((tm, 1), F32)],
        compiler_params=pltpu.CompilerParams(
            dimension_semantics=("arbitrary", "arbitrary"),
            vmem_limit_bytes=VMEM_LIMIT),
        name="in_projection",
    )(x2, pre_norm_w, w_main, w_small)


def _rope(x, cos_t, sin_t):
    half = ROPE_DIM // 2
    lane = lax.broadcasted_iota(jnp.int32, x.shape, 1)
    partner = jnp.where(lane < half, pltpu.roll(x, LANES - half, 1), pltpu.roll(x, half, 1))
    return x * cos_t + partner * sin_t


def _split_bf16(x):
    hi = x.astype(BF16)
    return hi, (x - hi.astype(F32)).astype(BF16)


def _moba_kernel(q_ref, k_ref, v_ref, z_ref, cos_ref, sin_ref, o_ref, qa_s, ka_s, va_s):
    s_len = k_ref.shape[0]
    n_blocks = s_len // MOBA_BLOCK
    blk_shift = MOBA_BLOCK.bit_length() - 1
    scale = HEAD_DIM ** -0.5 * 1.4426950408889634
    cos_t = cos_ref[...]
    sin_t = sin_ref[...]
    row = lax.broadcasted_iota(jnp.int32, (s_len, LANES), 0)
    lane = lax.broadcasted_iota(jnp.int32, (s_len, LANES), 1)
    blk_of_row = _shr(row, blk_shift)

    kr = _rope(k_ref[...].astype(F32), cos_t, sin_t)
    ka_s[:, :HEAD_DIM] = kr.astype(BF16)
    ka_s[:, HEAD_DIM:] = jnp.where(blk_of_row == lane, 1.0, 0.0).astype(BF16)
    va_s[:, :HEAD_DIM] = v_ref[...]
    va_s[:, HEAD_DIM:] = jnp.ones((s_len, HEAD_DIM), BF16)
    km = jnp.mean(kr.reshape(n_blocks, MOBA_BLOCK, HEAD_DIM), axis=1)
    gate_rows = -(-n_blocks // 16) * 16
    if gate_rows > n_blocks:
        km = jnp.concatenate([km, jnp.zeros((gate_rows - n_blocks, HEAD_DIM), F32)], axis=0)

    qr = _rope(q_ref[...].astype(F32), cos_t, sin_t)
    qa_s[:, :HEAD_DIM] = (qr * scale).astype(BF16)
    q_hi, q_lo = _split_bf16(qr)
    km_hi, km_lo = _split_bf16(km)
    gate2 = _dot_nt(jnp.concatenate([km_hi, km_lo], axis=0), q_hi)
    gate_t = gate2[:gate_rows, :] + gate2[gate_rows:, :] + _dot_nt(km_hi, q_lo)
    blk_f = lax.broadcasted_iota(jnp.int32, (gate_rows, s_len), 0).astype(F32)
    own_f = _shr(lax.broadcasted_iota(jnp.int32, (gate_rows, s_len), 1), blk_shift).astype(F32)
    g = jnp.where(blk_f < own_f, gate_t, GATE_MASKED)
    sel = jnp.where(blk_f == own_f, 1.0, 0.0)
    for _ in range(MOBA_TOPK):
        m = jnp.max(g, axis=0, keepdims=True)
        cand = jnp.where((g == m) & (g > 0.5 * GATE_MASKED), blk_f, float(LANES))
        idx = jnp.min(cand, axis=0, keepdims=True)
        pick = blk_f == idx
        sel = jnp.where(pick, 1.0, sel)
        g = jnp.where(pick, GATE_MASKED, g)
    bias_t = jnp.concatenate([jnp.where(sel > 0.5, 0.0, NEG),
                              jnp.zeros((LANES - gate_rows, s_len), F32)], axis=0)
    for j in range(s_len // LANES):
        qa_s[j * LANES:(j + 1) * LANES, HEAD_DIM:] = (
            bias_t[:, j * LANES:(j + 1) * LANES].T.astype(BF16))

    rr = lax.broadcasted_iota(jnp.int32, (MOBA_BLOCK, MOBA_BLOCK), 0)
    cc = lax.broadcasted_iota(jnp.int32, (MOBA_BLOCK, MOBA_BLOCK), 1)
    causal = cc <= rr
    for t in range(n_blocks):
        r0 = t * MOBA_BLOCK
        n_keys = r0 + MOBA_BLOCK
        qa = qa_s[r0:n_keys, :]
        s_own = jnp.where(causal, _dot_nt(qa, ka_s[r0:n_keys, :]), NEG)
        if t > 0:
            s_all = jnp.concatenate([_dot_nt(qa, ka_s[0:r0, :]), s_own], axis=1)
        else:
            s_all = s_own
        m = jnp.max(s_all, axis=1, keepdims=True)
        p = jnp.exp2(s_all - m)
        acc = _dot(p.astype(BF16), va_s[0:n_keys, :])
        o = (acc[:, :HEAD_DIM] * (1.0 / acc[:, HEAD_DIM:])
             * _silu(z_ref[r0:n_keys, :].astype(F32)))
        o_ref[r0:n_keys, :] = o.astype(o_ref.dtype)


def _moba(proj3, cos_t, sin_t):
    b, s, _ = proj3.shape
    seq_spec = lambda col: pl.BlockSpec((None, s, HEAD_DIM), lambda bi, h: (bi, 0, col + h))
    table = pl.BlockSpec((s, LANES), lambda bi, h: (0, 0))
    return pl.pallas_call(
        _moba_kernel,
        grid=(b, N_HEADS),
        in_specs=[seq_spec(COL_QA), seq_spec(COL_KA), seq_spec(COL_VA), seq_spec(COL_ZA),
                  table, table],
        out_specs=pl.BlockSpec((None, s, HEAD_DIM), lambda bi, h: (bi, 0, h)),
        out_shape=jax.ShapeDtypeStruct((b, s, N_HEADS * HEAD_DIM), BF16),
        scratch_shapes=[
            pltpu.VMEM((s, 2 * HEAD_DIM), BF16),
            pltpu.VMEM((s, 2 * HEAD_DIM), BF16),
            pltpu.VMEM((s, 2 * HEAD_DIM), BF16),
        ],
        compiler_params=pltpu.CompilerParams(
            dimension_semantics=("arbitrary", "arbitrary"),
            vmem_limit_bytes=VMEM_LIMIT),
        name="moba_attention",
    )(proj3, proj3, proj3, proj3, cos_t, sin_t)


GDN_PAIR = 2 * GDN_CHUNK
GDN_PAD = 32
GDN_ROWS = 512
GDN_SCAN_GROUP = 8
GDN_UNROLL = 32
GDN_UNROLL_LEVELS = 32


def _gdn_kernel(q_ref, k_ref, v_ref, z_ref, sm_ref, cwq_ref, cwk_ref, cwv_ref,
                alane_ref, dlane_ref, nw_ref, o_ref,
                pad_s, ball_s, gall_s, gt_s, b_s, g_s,
                qb_s, qe_s, kb_s, kbb_s, kdt_s, rhs_s,
                m_s, x_s, qk_s, wu_s, c_s, bb_s, qp_s, op_s, st_s, gm_s, e_s):
    h = pl.program_id(1)
    s_len = q_ref.shape[0]
    c = GDN_CHUNK
    pp = GDN_PAIR
    n_chunks = s_len // c
    n_pairs = s_len // pp
    n_tiles = s_len // GDN_ROWS
    chunk_shift = c.bit_length() - 1

    def rows_of(i, size):
        return pl.ds(pl.multiple_of(i * size, size), size)

    @pl.when(h == 0)
    def _():
        pad_s[0:GDN_PAD, :] = jnp.zeros((GDN_PAD, LANES), F32)
        sm = sm_ref[...]
        ball_s[...] = _sigmoid(sm)
        xs = sm + dlane_ref[...]
        softplus = jnp.maximum(xs, 0.0) + jnp.log(1.0 + jnp.exp(-jnp.abs(xs)))
        pad_s[GDN_PAD:, :] = -jnp.exp(alane_ref[...]) * softplus
        pos = lax.broadcasted_iota(jnp.int32, (s_len, LANES), 0) & (c - 1)
        shift = 1
        while shift < c:
            cur = pad_s[GDN_PAD:, :]
            prev = pad_s[pl.ds(GDN_PAD - shift, s_len), :]
            pad_s[GDN_PAD:, :] = cur + jnp.where(pos >= shift, prev, 0.0)
            shift *= 2
        gall_s[...] = pad_s[GDN_PAD:, :]

        def tr(j, carry):
            gt_s[j] = gall_s[rows_of(j, pp), :].T
            return carry
        lax.fori_loop(0, n_pairs, tr, 0)

    def conv_silu_tile(i, cw):
        base = pl.multiple_of(i * GDN_ROWS, GDN_ROWS) + GDN_PAD
        y = pad_s[pl.ds(base, GDN_ROWS), :] * cw[CONV_WIDTH - 1:CONV_WIDTH, :]
        for back in range(1, CONV_WIDTH):
            y = y + (pad_s[pl.ds(base - back, GDN_ROWS), :]
                     * cw[CONV_WIDTH - 1 - back:CONV_WIDTH - back, :])
        return _silu(y)

    def l2n(x):
        return x * lax.rsqrt(jnp.sum(x * x, axis=-1, keepdims=True) + EPS)

    lane_t = lax.broadcasted_iota(jnp.int32, (GDN_ROWS, LANES), 1)

    def head_column(all_s, r, lane_idx):
        col = jnp.sum(jnp.where(lane_t == lane_idx, all_s[r, :], 0.0), axis=1, keepdims=True)
        return jnp.broadcast_to(col, (GDN_ROWS, LANES))

    pad_s[GDN_PAD:, :] = v_ref[...].astype(F32)
    cwv = cwv_ref[...]

    def v_tile(i, carry):
        r = rows_of(i, GDN_ROWS)
        bt = head_column(ball_s, r, h + LANE_BETA)
        b_s[r, :] = bt
        rhs_s[r, HEAD_DIM:2 * HEAD_DIM] = (conv_silu_tile(i, cwv) * bt).astype(BF16)
        return carry
    lax.fori_loop(0, n_tiles, v_tile, 0)

    pad_s[GDN_PAD:, :] = k_ref[...].astype(F32)
    cwk = cwk_ref[...]

    def k_tile(i, carry):
        r = rows_of(i, GDN_ROWS)
        gc = head_column(gall_s, r, h + LANE_DECAY)
        g_s[r, :] = gc
        kn = l2n(conv_silu_tile(i, cwk))
        kbeta = kn * b_s[r, :]
        kb_s[r, :] = kn.astype(BF16)
        kbb_s[r, :] = kbeta.astype(BF16)
        rhs_s[r, 0:HEAD_DIM] = (kbeta * jnp.exp(gc)).astype(BF16)
        g3 = gc.reshape(GDN_ROWS // c, c, LANES)
        kd = (kn.reshape(GDN_ROWS // c, c, LANES) * jnp.exp(g3[:, c - 1:c, :] - g3)
              ).reshape(GDN_ROWS, LANES)
        for p in range(GDN_ROWS // pp):
            kdt_s[i * (GDN_ROWS // pp) + p] = kd[p * pp:(p + 1) * pp, :].T.astype(BF16)
        return carry
    lax.fori_loop(0, n_tiles, k_tile, 0)

    pad_s[GDN_PAD:, :] = q_ref[...].astype(F32)
    cwq = cwq_ref[...]

    def q_tile(i, carry):
        r = rows_of(i, GDN_ROWS)
        y = conv_silu_tile(i, cwq)
        qn = y * (lax.rsqrt(jnp.sum(y * y, axis=-1, keepdims=True) + EPS) * (HEAD_DIM ** -0.5))
        qb_s[r, :] = qn.astype(BF16)
        qe_s[r, :] = qn * jnp.exp(g_s[r, :])
        return carry
    lax.fori_loop(0, n_tiles, q_tile, 0)

    ri = lax.broadcasted_iota(jnp.int32, (pp, pp), 0)
    ci = lax.broadcasted_iota(jnp.int32, (pp, pp), 1)
    same_chunk = _shr(ri, chunk_shift) == _shr(ci, chunk_shift)
    causal = same_chunk & (ci <= ri)
    strict_f = jnp.where(same_chunk & (ci < ri), 1.0, 0.0)
    eye = jnp.where(ci == ri, 1.0, 0.0)
    level_masks = []
    size = 1
    while size < c:
        sh = size.bit_length() - 1
        same_big = _shr(ri, sh + 1) == _shr(ci, sh + 1)
        diff_small = _shr(ri, sh) != _shr(ci, sh)
        level_masks.append(jnp.where(same_big & diff_small & (ci < ri), 1.0, 0.0))
        size *= 2

    def gram(j, carry):
        r = rows_of(j, pp)
        g_col = g_s[r, :]
        g_row = gt_s[j, pl.ds(h + LANE_DECAY, 1), :]
        diff = g_col - g_row
        decay = jnp.where(causal, jnp.exp(jnp.where(causal, diff, 0.0)), 0.0)
        kk = kb_s[r, :]
        m = _dot_nt(kbb_s[r, :], kk) * decay * strict_f
        qk_s[r, :] = (_dot_nt(qb_s[r, :], kk) * decay).astype(BF16)
        m_s[r, :] = m
        x_s[r, :] = eye - m * level_masks[0]
        return carry
    lax.fori_loop(0, n_pairs, gram, 0, unroll=min(n_pairs, GDN_UNROLL))

    size = 1
    for lm in level_masks[1:]:
        size *= 2
        if size % 8 == 0:
            def level(j, carry, lm=lm, size=size):
                base = pl.multiple_of(j * pp, pp)
                r = pl.ds(base, pp)
                odd_rows = [pl.ds(base + (2 * b + 1) * size, size) for b in range(pp // (2 * size))]
                xo = jnp.concatenate([x_s[rr, :] for rr in odd_rows], axis=0)
                y = _dot(xo.astype(BF16), (m_s[r, :] * lm).astype(BF16))
                new = xo - _dot(y.astype(BF16), x_s[r, :].astype(BF16))
                for b, rr in enumerate(odd_rows):
                    x_s[rr, :] = new[b * size:(b + 1) * size, :]
                return carry
        else:
            def level(j, carry, lm=lm):
                r = rows_of(j, pp)
                x = x_s[r, :]
                xb = x.astype(BF16)
                y = _dot(xb, (m_s[r, :] * lm).astype(BF16))
                x_s[r, :] = x - _dot(y.astype(BF16), xb)
                return carry
        lax.fori_loop(0, n_pairs, level, 0, unroll=min(n_pairs, GDN_UNROLL_LEVELS))

    def solve(j, carry):
        r = rows_of(j, pp)
        wu_s[r, :] = _dot(x_s[r, :].astype(BF16), rhs_s[r, :]).astype(BF16)
        return carry
    lax.fori_loop(0, n_pairs, solve, 0, unroll=min(n_pairs, GDN_UNROLL_LEVELS))

    def maps(j, carry):
        r = rows_of(j, pp)
        wu = wu_s[r, :]
        qo = _dot(qk_s[r, :], wu)
        qp_s[r, :] = (qe_s[r, :] - qo[:, 0:HEAD_DIM]).astype(BF16)
        op_s[r, :] = qo[:, HEAD_DIM:2 * HEAD_DIM]
        kdt = kdt_s[j]
        for half in range(2):
            cb = _dot(kdt[:, half * c:(half + 1) * c], wu[half * c:(half + 1) * c, :])
            c_s[2 * j + half] = (-cb[:, 0:HEAD_DIM]).astype(BF16)
            bb_s[2 * j + half] = cb[:, HEAD_DIM:2 * HEAD_DIM]
        return carry
    lax.fori_loop(0, n_pairs, maps, 0, unroll=min(n_pairs, GDN_UNROLL))

    grp = GDN_SCAN_GROUP
    n_groups = n_chunks // grp

    def glast_row(n):
        return g_s[pl.ds(n * c + (c - 1), 1), :]

    def first(gi, carry):
        n0 = gi * grp
        gm_s[pl.ds(n0, 1), :] = glast_row(n0)
        return carry
    lax.fori_loop(0, n_groups, first, 0, unroll=min(n_groups, GDN_UNROLL))

    for jj in range(1, grp):
        def compose(gi, carry, jj=jj):
            n = gi * grp + jj
            gsum = gm_s[pl.ds(n - 1, 1), :]
            glj = glast_row(n)
            n_mat = c_s[n]
            p_prev = c_s[n - 1]
            q_prev = bb_s[n - 1]
            res = _dot(n_mat, jnp.concatenate([p_prev, q_prev.astype(BF16)], axis=1))
            p_new = (jnp.exp(glj) * p_prev.astype(F32) + jnp.exp(gsum) * n_mat.astype(F32)
                     + res[:, 0:HEAD_DIM])
            c_s[n] = p_new.astype(BF16)
            bb_s[n] = jnp.exp(glj) * q_prev + res[:, HEAD_DIM:2 * HEAD_DIM] + bb_s[n]
            gm_s[pl.ds(n, 1), :] = gsum + glj
            return carry
        lax.fori_loop(0, n_groups, compose, 0, unroll=min(n_groups, GDN_UNROLL))

    gm_s[...] = jnp.exp(gm_s[...])

    def chain(gi, e):
        eb = e.astype(BF16)
        e_s[gi] = e
        st_s[gi * grp] = eb
        nl = gi * grp + (grp - 1)
        return e * gm_s[pl.ds(nl, 1), :] + _dot(c_s[nl], eb) + bb_s[nl]
    lax.fori_loop(0, n_groups, chain, jnp.zeros((HEAD_DIM, HEAD_DIM), F32))

    def fill(gi, carry):
        e = e_s[gi]
        eb = st_s[gi * grp]
        for jj in range(1, grp):
            n = gi * grp + jj
            st_s[n] = (e * gm_s[pl.ds(n - 1, 1), :] + _dot(c_s[n - 1], eb) + bb_s[n - 1]).astype(BF16)
        return carry
    lax.fori_loop(0, n_groups, fill, 0, unroll=min(n_groups, GDN_UNROLL))

    nw = nw_ref[...]

    def emit(n, carry):
        r = rows_of(n, c)
        o = _dot(qp_s[r, :], st_s[n]) + op_s[r, :]
        o = o * lax.rsqrt(jnp.mean(o * o, axis=-1, keepdims=True) + EPS) * nw
        o_ref[r, :] = (o * _silu(z_ref[r, :].astype(F32))).astype(o_ref.dtype)
        return carry
    lax.fori_loop(0, n_chunks, emit, 0, unroll=min(n_chunks, GDN_UNROLL))


def _gdn(proj3, small3, conv_w, alane, dlane, gdn_norm_w):
    b, s, _ = proj3.shape
    c = GDN_CHUNK
    seq_spec = lambda col: pl.BlockSpec((None, s, HEAD_DIM), lambda bi, h: (bi, 0, col + h))
    cw_spec = lambda off: pl.BlockSpec((CONV_WIDTH, HEAD_DIM), lambda bi, h: (0, off + h))
    row_spec = pl.BlockSpec((1, LANES), lambda bi, h: (0, 0))
    seq = lambda dt, w=HEAD_DIM: pltpu.VMEM((s, w), dt)
    return pl.pallas_call(
        _gdn_kernel,
        grid=(b, N_HEADS),
        in_specs=[
            seq_spec(COL_QB), seq_spec(COL_KB), seq_spec(COL_VB), seq_spec(COL_ZB),
            pl.BlockSpec((None, s, LANES), lambda bi, h: (bi, 0, 0)),
            cw_spec(0), cw_spec(N_HEADS), cw_spec(2 * N_HEADS),
            row_spec, row_spec, row_spec,
        ],
        out_specs=pl.BlockSpec((None, s, HEAD_DIM), lambda bi, h: (bi, 0, h)),
        out_shape=jax.ShapeDtypeStruct((b, s, N_HEADS * HEAD_DIM), BF16),
        scratch_shapes=[
            pltpu.VMEM((s + GDN_PAD, LANES), F32),
            seq(F32), seq(F32),
            pltpu.VMEM((s // GDN_PAIR, LANES, GDN_PAIR), F32),
            seq(F32), seq(F32),
            seq(BF16), seq(F32),
            seq(BF16), seq(BF16),
            pltpu.VMEM((s // GDN_PAIR, HEAD_DIM, GDN_PAIR), BF16),
            seq(BF16, 2 * HEAD_DIM),
            seq(F32), seq(F32), seq(BF16),
            seq(BF16, 2 * HEAD_DIM),
            pltpu.VMEM((s // c, HEAD_DIM, HEAD_DIM), BF16),
            pltpu.VMEM((s // c, HEAD_DIM, HEAD_DIM), F32),
            seq(BF16), seq(F32),
            pltpu.VMEM((s // c, HEAD_DIM, HEAD_DIM), BF16),
            pltpu.VMEM((s // c, LANES), F32),
            pltpu.VMEM((s // c // GDN_SCAN_GROUP, HEAD_DIM, HEAD_DIM), F32),
        ],
        compiler_params=pltpu.CompilerParams(
            dimension_semantics=("arbitrary", "arbitrary"),
            vmem_limit_bytes=VMEM_LIMIT),
        name="gated_deltanet",
    )(proj3, proj3, proj3, proj3, small3, conv_w, conv_w, conv_w, alane, dlane, gdn_norm_w)


def _out_kernel(x_ref, a_ref, b_ref, ga_ref, gb_ref, wa_ref, wb_ref, wo_ref, nw_ref, o_ref):
    y_a = _dot(a_ref[...], wa_ref[...])
    y_b = _dot(b_ref[...], wb_ref[...])
    merged = (_sigmoid(ga_ref[...].astype(F32)) * y_a
              + _sigmoid(gb_ref[...].astype(F32)) * y_b)
    out = _dot(merged.astype(BF16), wo_ref[...])
    ms = jnp.mean(out * out, axis=-1, keepdims=True)
    o_ref[...] = x_ref[...] + out * lax.rsqrt(ms + EPS) * nw_ref[...]


def _out_projection(x2, a2, b2, proj2, wa, wb, wo, post_norm_w, tm=512):
    t = x2.shape[0]
    tok = lambda: pl.BlockSpec((tm, D_MODEL), lambda i: (i, 0))
    full = lambda: pl.BlockSpec((D_MODEL, D_MODEL), lambda i: (0, 0))
    return pl.pallas_call(
        _out_kernel,
        grid=(t // tm,),
        in_specs=[
            tok(), tok(), tok(),
            pl.BlockSpec((tm, D_MODEL), lambda i: (i, BLK_GATE_A)),
            pl.BlockSpec((tm, D_MODEL), lambda i: (i, BLK_GATE_B)),
            full(), full(), full(),
            pl.BlockSpec((1, D_MODEL), lambda i: (0, 0)),
        ],
        out_specs=tok(),
        out_shape=jax.ShapeDtypeStruct((t, D_MODEL), F32),
        compiler_params=pltpu.CompilerParams(
            dimension_semantics=("arbitrary",),
            vmem_limit_bytes=VMEM_LIMIT),
        name="out_projection",
    )(x2, a2, b2, proj2, proj2, wa, wb, wo, post_norm_w)


def _rope_tables(s):
    half = ROPE_DIM // 2
    inv_freq = np.power(ROPE_THETA, -np.arange(half, dtype=np.float64) * (2.0 / ROPE_DIM))
    ang = np.arange(s, dtype=np.float64)[:, None] * inv_freq[None, :]
    cos, sin = np.cos(ang), np.sin(ang)
    ones = np.ones((s, HEAD_DIM - ROPE_DIM))
    zeros = np.zeros((s, HEAD_DIM - ROPE_DIM))
    cos_t = np.concatenate([cos, cos, ones], axis=1).astype(np.float32)
    sin_t = np.concatenate([-sin, sin, zeros], axis=1).astype(np.float32)
    return jnp.asarray(cos_t), jnp.asarray(sin_t)


def _head_lanes(v):
    return jnp.pad(v.astype(F32), (LANE_DECAY, LANES - LANE_DECAY - N_HEADS)).reshape(1, LANES)


def _layer(x, pre_norm_w, w_in, conv_w, a_log, dt_bias, gdn_norm_w,
           w_branch_a, w_branch_b, w_out, post_norm_w):
    b, s, d = x.shape
    x2 = x.reshape(b * s, d)
    n_main = 8 * D_MODEL
    w_main = jnp.concatenate([w_in[:, :n_main], w_in[:, n_main + 2 * N_HEADS:]], axis=1).astype(BF16)
    w_small = jnp.pad(w_in[:, n_main:n_main + 2 * N_HEADS],
                      ((0, 0), (0, LANES - 2 * N_HEADS))).astype(BF16)

    proj, small = _in_projection(x2, pre_norm_w.reshape(1, d), w_main, w_small)
    proj3 = proj.reshape(b, s, N_MAIN_BLOCKS * D_MODEL)
    small3 = small.reshape(b, s, LANES)

    cos_t, sin_t = _rope_tables(s)
    act_a = _moba(proj3, cos_t, sin_t)
    act_b = _gdn(proj3, small3, conv_w, _head_lanes(a_log), _head_lanes(dt_bias),
                 gdn_norm_w.reshape(1, HEAD_DIM))

    out = _out_projection(x2, act_a.reshape(b * s, d), act_b.reshape(b * s, d), proj,
                          w_branch_a.astype(BF16), w_branch_b.astype(BF16), w_out.astype(BF16),
                          post_norm_w.reshape(1, d))
    return out.reshape(b, s, d)


def kernel(x, pre_norm_w, w_in, conv_w, a_log, dt_bias, gdn_norm_w,
           w_branch_a, w_branch_b, w_out, post_norm_w):
    for layer in range(pre_norm_w.shape[0]):
        x = _layer(x, pre_norm_w[layer], w_in[layer], conv_w[layer], a_log[layer],
                   dt_bias[layer], gdn_norm_w[layer], w_branch_a[layer],
                   w_branch_b[layer], w_out[layer], post_norm_w[layer])
    return x
```

```python
import jax
import jax.numpy as jnp
import numpy as np
from jax import lax
from jax.experimental import pallas as pl
from jax.experimental.pallas import tpu as pltpu

F32 = jnp.float32
BF16 = jnp.bfloat16

D_MODEL = 1024
N_HEADS = 8
HEAD_DIM = 128
MOBA_BLOCK = 256
MOBA_TOPK = 3
ROPE_THETA = 500000.0
ROPE_DIM = HEAD_DIM // 4
CONV_WIDTH = 4
GDN_CHUNK = 64
EPS = 1e-6
NEG = -1e30
GATE_MASKED = -3.0e38
LANES = 128
N_MAIN_BLOCKS = 10

COL_QA, COL_KA, COL_VA, COL_ZA = 0, 8, 16, 24
COL_QB, COL_KB, COL_VB, COL_ZB = 32, 40, 48, 56
BLK_GATE_A, BLK_GATE_B = 8, 9
LANE_BETA, LANE_DECAY = 0, N_HEADS

VMEM_LIMIT = 58 * 1024 * 1024
HIGHEST = lax.Precision.HIGHEST


def _dot(a, b, precision=None):
    return jnp.dot(a, b, preferred_element_type=F32, precision=precision)


def _dot_nt(a, b, precision=None):
    return lax.dot_general(a, b, (((1,), (1,)), ((), ())),
                           preferred_element_type=F32, precision=precision)


def _dot_tn(a, b, precision=None):
    return lax.dot_general(a, b, (((0,), (0,)), ((), ())),
                           preferred_element_type=F32, precision=precision)


def _silu(x):
    return x * (1.0 / (1.0 + jnp.exp(-x)))


def _sigmoid(x):
    return 1.0 / (1.0 + jnp.exp(-x))


def _shr(x, n):
    return lax.shift_right_logical(x, n)


def _proj_kernel(x_ref, nw_ref, w_ref, ws_ref, o_ref, os_ref, h_ref, inv_ref):
    j = pl.program_id(1)

    @pl.when(j == 0)
    def _():
        x = x_ref[...]
        inv = lax.rsqrt(jnp.mean(x * x, axis=-1, keepdims=True) + EPS)
        inv_ref[...] = inv
        hb = (x * nw_ref[...]).astype(BF16)
        h_ref[...] = hb
        os_ref[...] = _dot(hb, ws_ref[...]) * inv

    o_ref[...] = (_dot(h_ref[...], w_ref[...]) * inv_ref[...]).astype(o_ref.dtype)


def _in_projection(x2, pre_norm_w, w_main, w_small, tm=2048, tn=1024):
    t = x2.shape[0]
    n_cols = w_main.shape[1]
    return pl.pallas_call(
        _proj_kernel,
        grid=(t // tm, n_cols // tn),
        in_specs=[
            pl.BlockSpec((tm, D_MODEL), lambda i, j: (i, 0)),
            pl.BlockSpec((1, D_MODEL), lambda i, j: (0, 0)),
            pl.BlockSpec((D_MODEL, tn), lambda i, j: (0, j)),
            pl.BlockSpec((D_MODEL, LANES), lambda i, j: (0, 0)),
        ],
        out_specs=[
            pl.BlockSpec((tm, tn), lambda i, j: (i, j)),
            pl.BlockSpec((tm, LANES), lambda i, j: (i, 0)),
        ],
        out_shape=[
            jax.ShapeDtypeStruct((t, n_cols), BF16),
            jax.ShapeDtypeStruct((t, LANES), F32),
        ],
        scratch_shapes=[pltpu.VMEM((tm, D_MODEL), BF16), pltpu.VMEM((tm, 1), F32)],
        compiler_params=pltpu.CompilerParams(
            dimension_semantics=("arbitrary", "arbitrary"),
            vmem_limit_bytes=VMEM_LIMIT),
        name="in_projection",
    )(x2, pre_norm_w, w_main, w_small)


def _rope(x, cos_t, sin_t):
    half = ROPE_DIM // 2
    lane = lax.broadcasted_iota(jnp.int32, x.shape, 1)
    partner = jnp.where(lane < half, pltpu.roll(x, LANES - half, 1), pltpu.roll(x, half, 1))
    return x * cos_t + partner * sin_t


def _split_bf16(x):
    hi = x.astype(BF16)
    return hi, (x - hi.astype(F32)).astype(BF16)


def _moba_kernel(q_ref, k_ref, v_ref, z_ref, cos_ref, sin_ref, o_ref, qa_s, ka_s, va_s):
    s_len = k_ref.shape[0]
    n_blocks = s_len // MOBA_BLOCK
    blk_shift = MOBA_BLOCK.bit_length() - 1
    scale = HEAD_DIM ** -0.5 * 1.4426950408889634
    cos_t = cos_ref[...]
    sin_t = sin_ref[...]
    row = lax.broadcasted_iota(jnp.int32, (s_len, LANES), 0)
    lane = lax.broadcasted_iota(jnp.int32, (s_len, LANES), 1)
    blk_of_row = _shr(row, blk_shift)

    kr = _rope(k_ref[...].astype(F32), cos_t, sin_t)
    ka_s[:, :HEAD_DIM] = kr.astype(BF16)
    ka_s[:, HEAD_DIM:] = jnp.where(blk_of_row == lane, 1.0, 0.0).astype(BF16)
    va_s[:, :HEAD_DIM] = v_ref[...]
    va_s[:, HEAD_DIM:] = jnp.ones((s_len, HEAD_DIM), BF16)
    km = jnp.mean(kr.reshape(n_blocks, MOBA_BLOCK, HEAD_DIM), axis=1)
    gate_rows = -(-n_blocks // 16) * 16
    if gate_rows > n_blocks:
        km = jnp.concatenate([km, jnp.zeros((gate_rows - n_blocks, HEAD_DIM), F32)], axis=0)

    qr = _rope(q_ref[...].astype(F32), cos_t, sin_t)
    qa_s[:, :HEAD_DIM] = (qr * scale).astype(BF16)
    q_hi, q_lo = _split_bf16(qr)
    km_hi, km_lo = _split_bf16(km)
    gate2 = _dot_nt(jnp.concatenate([km_hi, km_lo], axis=0), q_hi)
    gate_t = gate2[:gate_rows, :] + gate2[gate_rows:, :] + _dot_nt(km_hi, q_lo)
    blk_f = lax.broadcasted_iota(jnp.int32, (gate_rows, s_len), 0).astype(F32)
    own_f = _shr(lax.broadcasted_iota(jnp.int32, (gate_rows, s_len), 1), blk_shift).astype(F32)
    g = jnp.where(blk_f < own_f, gate_t, GATE_MASKED)
    sel = jnp.where(blk_f == own_f, 1.0, 0.0)
    for _ in range(MOBA_TOPK):
        m = jnp.max(g, axis=0, keepdims=True)
        cand = jnp.where((g == m) & (g > 0.5 * GATE_MASKED), blk_f, float(LANES))
        idx = jnp.min(cand, axis=0, keepdims=True)
        pick = blk_f == idx
        sel = jnp.where(pick, 1.0, sel)
        g = jnp.where(pick, GATE_MASKED, g)
    bias_t = jnp.concatenate([jnp.where(sel > 0.5, 0.0, NEG),
                              jnp.zeros((LANES - gate_rows, s_len), F32)], axis=0)
    for j in range(s_len // LANES):
        qa_s[j * LANES:(j + 1) * LANES, HEAD_DIM:] = (
            bias_t[:, j * LANES:(j + 1) * LANES].T.astype(BF16))

    rr = lax.broadcasted_iota(jnp.int32, (MOBA_BLOCK, MOBA_BLOCK), 0)
    cc = lax.broadcasted_iota(jnp.int32, (MOBA_BLOCK, MOBA_BLOCK), 1)
    causal = cc <= rr
    for t in reversed(range(n_blocks)):
        r0 = t * MOBA_BLOCK
        n_keys = r0 + MOBA_BLOCK
        qa = qa_s[r0:n_keys, :]
        s_own = jnp.where(causal, _dot_nt(qa, ka_s[r0:n_keys, :]), NEG)
        if t > 0:
            s_all = jnp.concatenate([_dot_nt(qa, ka_s[0:r0, :]), s_own], axis=1)
        else:
            s_all = s_own
        m = jnp.max(s_all, axis=1, keepdims=True)
        p = jnp.exp2(s_all - m)
        acc = _dot(p.astype(BF16), va_s[0:n_keys, :])
        o = (acc[:, :HEAD_DIM] * (1.0 / acc[:, HEAD_DIM:])
             * _silu(z_ref[r0:n_keys, :].astype(F32)))
        o_ref[r0:n_keys, :] = o.astype(o_ref.dtype)


def _moba(proj3, cos_t, sin_t):
    b, s, _ = proj3.shape
    seq_spec = lambda col: pl.BlockSpec((None, s, HEAD_DIM), lambda bi, h: (bi, 0, col + h))
    table = pl.BlockSpec((s, LANES), lambda bi, h: (0, 0))
    return pl.pallas_call(
        _moba_kernel,
        grid=(b, N_HEADS),
        in_specs=[seq_spec(COL_QA), seq_spec(COL_KA), seq_spec(COL_VA), seq_spec(COL_ZA),
                  table, table],
        out_specs=pl.BlockSpec((None, s, HEAD_DIM), lambda bi, h: (bi, 0, h)),
        out_shape=jax.ShapeDtypeStruct((b, s, N_HEADS * HEAD_DIM), BF16),
        scratch_shapes=[
            pltpu.VMEM((s, 2 * HEAD_DIM), BF16),
            pltpu.VMEM((s, 2 * HEAD_DIM), BF16),
            pltpu.VMEM((s, 2 * HEAD_DIM), BF16),
        ],
        compiler_params=pltpu.CompilerParams(
            dimension_semantics=("arbitrary", "arbitrary"),
            vmem_limit_bytes=VMEM_LIMIT),
        name="moba_attention",
    )(proj3, proj3, proj3, proj3, cos_t, sin_t)


GDN_PAIR = 2 * GDN_CHUNK
GDN_PAD = 32
GDN_ROWS = 512
GDN_SCAN_GROUP = 8
GDN_UNROLL = 32
GDN_UNROLL_LEVELS = 32


def _gdn_kernel(q_ref, k_ref, v_ref, z_ref, sm_ref, cwq_ref, cwk_ref, cwv_ref,
                alane_ref, dlane_ref, nw_ref, o_ref,
                pad_s, ball_s, gall_s, gt_s, b_s, g_s,
                qb_s, qe_s, kb_s, kbb_s, kdt_s, rhs_s,
                m_s, x_s, qk_s, wu_s, c_s, bb_s, qp_s, op_s, st_s, gm_s, e_s):
    h = pl.program_id(1)
    s_len = q_ref.shape[0]
    c = GDN_CHUNK
    pp = GDN_PAIR
    n_chunks = s_len // c
    n_pairs = s_len // pp
    n_tiles = s_len // GDN_ROWS
    chunk_shift = c.bit_length() - 1

    def rows_of(i, size):
        return pl.ds(pl.multiple_of(i * size, size), size)

    @pl.when(h == 0)
    def _():
        pad_s[0:GDN_PAD, :] = jnp.zeros((GDN_PAD, LANES), F32)
        sm = sm_ref[...]
        ball_s[...] = _sigmoid(sm)
        xs = sm + dlane_ref[...]
        softplus = jnp.maximum(xs, 0.0) + jnp.log(1.0 + jnp.exp(-jnp.abs(xs)))
        pad_s[GDN_PAD:, :] = -jnp.exp(alane_ref[...]) * softplus
        pos = lax.broadcasted_iota(jnp.int32, (s_len, LANES), 0) & (c - 1)
        shift = 1
        while shift < c:
            cur = pad_s[GDN_PAD:, :]
            prev = pad_s[pl.ds(GDN_PAD - shift, s_len), :]
            pad_s[GDN_PAD:, :] = cur + jnp.where(pos >= shift, prev, 0.0)
            shift *= 2
        gall_s[...] = pad_s[GDN_PAD:, :]

        def tr(j, carry):
            gt_s[j] = gall_s[rows_of(j, pp), :].T
            return carry
        lax.fori_loop(0, n_pairs, tr, 0)

    def conv_silu_tile(i, cw):
        base = pl.multiple_of(i * GDN_ROWS, GDN_ROWS) + GDN_PAD
        y = pad_s[pl.ds(base, GDN_ROWS), :] * cw[CONV_WIDTH - 1:CONV_WIDTH, :]
        for back in range(1, CONV_WIDTH):
            y = y + (pad_s[pl.ds(base - back, GDN_ROWS), :]
                     * cw[CONV_WIDTH - 1 - back:CONV_WIDTH - back, :])
        return _silu(y)

    def l2n(x):
        return x * lax.rsqrt(jnp.sum(x * x, axis=-1, keepdims=True) + EPS)

    lane_t = lax.broadcasted_iota(jnp.int32, (GDN_ROWS, LANES), 1)

    def head_column(all_s, r, lane_idx):
        col = jnp.sum(jnp.where(lane_t == lane_idx, all_s[r, :], 0.0), axis=1, keepdims=True)
        return jnp.broadcast_to(col, (GDN_ROWS, LANES))

    pad_s[GDN_PAD:, :] = v_ref[...].astype(F32)
    cwv = cwv_ref[...]

    def v_tile(i, carry):
        r = rows_of(i, GDN_ROWS)
        bt = head_column(ball_s, r, h + LANE_BETA)
        b_s[r, :] = bt
        rhs_s[r, HEAD_DIM:2 * HEAD_DIM] = (conv_silu_tile(i, cwv) * bt).astype(BF16)
        return carry
    lax.fori_loop(0, n_tiles, v_tile, 0)

    pad_s[GDN_PAD:, :] = k_ref[...].astype(F32)
    cwk = cwk_ref[...]

    def k_tile(i, carry):
        r = rows_of(i, GDN_ROWS)
        gc = head_column(gall_s, r, h + LANE_DECAY)
        g_s[r, :] = gc
        kn = l2n(conv_silu_tile(i, cwk))
        kbeta = kn * b_s[r, :]
        kb_s[r, :] = kn.astype(BF16)
        kbb_s[r, :] = kbeta.astype(BF16)
        rhs_s[r, 0:HEAD_DIM] = (kbeta * jnp.exp(gc)).astype(BF16)
        g3 = gc.reshape(GDN_ROWS // c, c, LANES)
        kd = (kn.reshape(GDN_ROWS // c, c, LANES) * jnp.exp(g3[:, c - 1:c, :] - g3)
              ).reshape(GDN_ROWS, LANES)
        for p in range(GDN_ROWS // pp):
            kdt_s[i * (GDN_ROWS // pp) + p] = kd[p * pp:(p + 1) * pp, :].T.astype(BF16)
        return carry
    lax.fori_loop(0, n_tiles, k_tile, 0)

    pad_s[GDN_PAD:, :] = q_ref[...].astype(F32)
    cwq = cwq_ref[...]

    def q_tile(i, carry):
        r = rows_of(i, GDN_ROWS)
        y = conv_silu_tile(i, cwq)
        qn = y * (lax.rsqrt(jnp.sum(y * y, axis=-1, keepdims=True) + EPS) * (HEAD_DIM ** -0.5))
        qb_s[r, :] = qn.astype(BF16)
        qe_s[r, :] = qn * jnp.exp(g_s[r, :])
        return carry
    lax.fori_loop(0, n_tiles, q_tile, 0)

    ri = lax.broadcasted_iota(jnp.int32, (pp, pp), 0)
    ci = lax.broadcasted_iota(jnp.int32, (pp, pp), 1)
    same_chunk = _shr(ri, chunk_shift) == _shr(ci, chunk_shift)
    causal = same_chunk & (ci <= ri)
    strict_f = jnp.where(same_chunk & (ci < ri), 1.0, 0.0)
    eye = jnp.where(ci == ri, 1.0, 0.0)
    level_masks = []
    size = 1
    while size < c:
        sh = size.bit_length() - 1
        same_big = _shr(ri, sh + 1) == _shr(ci, sh + 1)
        diff_small = _shr(ri, sh) != _shr(ci, sh)
        level_masks.append(jnp.where(same_big & diff_small & (ci < ri), 1.0, 0.0))
        size *= 2

    def gram(j, carry):
        r = rows_of(j, pp)
        g_col = g_s[r, :]
        g_row = gt_s[j, pl.ds(h + LANE_DECAY, 1), :]
        diff = g_col - g_row
        decay = jnp.where(causal, jnp.exp(jnp.where(causal, diff, 0.0)), 0.0)
        kk = kb_s[r, :]
        m = _dot_nt(kbb_s[r, :], kk) * decay * strict_f
        qk_s[r, :] = (_dot_nt(qb_s[r, :], kk) * decay).astype(BF16)
        m_s[r, :] = m
        x_s[r, :] = eye - m * level_masks[0]
        return carry
    lax.fori_loop(0, n_pairs, gram, 0, unroll=min(n_pairs, GDN_UNROLL))

    size = 1
    for lm in level_masks[1:]:
        size *= 2
        if size % 8 == 0:
            def level(j, carry, lm=lm, size=size):
                base = pl.multiple_of(j * pp, pp)
                r = pl.ds(base, pp)
                odd_rows = [pl.ds(base + (2 * b + 1) * size, size) for b in range(pp // (2 * size))]
                xo = jnp.concatenate([x_s[rr, :] for rr in odd_rows], axis=0)
                y = _dot(xo.astype(BF16), (m_s[r, :] * lm).astype(BF16))
                new = xo - _dot(y.astype(BF16), x_s[r, :].astype(BF16))
                for b, rr in enumerate(odd_rows):
                    x_s[rr, :] = new[b * size:(b + 1) * size, :]
                return carry
        else:
            def level(j, carry, lm=lm):
                r = rows_of(j, pp)
                x = x_s[r, :]
                xb = x.astype(BF16)
                y = _dot(xb, (m_s[r, :] * lm).astype(BF16))
                x_s[r, :] = x - _dot(y.astype(BF16), xb)
                return carry
        lax.fori_loop(0, n_pairs, level, 0, unroll=min(n_pairs, GDN_UNROLL_LEVELS))

    def solve(j, carry):
        r = rows_of(j, pp)
        wu_s[r, :] = _dot(x_s[r, :].astype(BF16), rhs_s[r, :]).astype(BF16)
        return carry
    lax.fori_loop(0, n_pairs, solve, 0, unroll=min(n_pairs, GDN_UNROLL_LEVELS))

    def maps(j, carry):
        r = rows_of(j, pp)
        wu = wu_s[r, :]
        qo = _dot(qk_s[r, :], wu)
        qp_s[r, :] = (qe_s[r, :] - qo[:, 0:HEAD_DIM]).astype(BF16)
        op_s[r, :] = qo[:, HEAD_DIM:2 * HEAD_DIM]
        kdt = kdt_s[j]
        for half in range(2):
            cb = _dot(kdt[:, half * c:(half + 1) * c], wu[half * c:(half + 1) * c, :])
            c_s[2 * j + half] = (-cb[:, 0:HEAD_DIM]).astype(BF16)
            bb_s[2 * j + half] = cb[:, HEAD_DIM:2 * HEAD_DIM]
        return carry
    lax.fori_loop(0, n_pairs, maps, 0, unroll=min(n_pairs, GDN_UNROLL))

    grp = GDN_SCAN_GROUP
    n_groups = n_chunks // grp

    def glast_row(n):
        return g_s[pl.ds(n * c + (c - 1), 1), :]

    def first(gi, carry):
        n0 = gi * grp
        gm_s[pl.ds(n0, 1), :] = glast_row(n0)
        return carry
    lax.fori_loop(0, n_groups, first, 0, unroll=min(n_groups, GDN_UNROLL))

    for jj in range(1, grp):
        def compose(gi, carry, jj=jj):
            n = gi * grp + jj
            gsum = gm_s[pl.ds(n - 1, 1), :]
            glj = glast_row(n)
            n_mat = c_s[n]
            p_prev = c_s[n - 1]
            q_prev = bb_s[n - 1]
            res = _dot(n_mat, jnp.concatenate([p_prev, q_prev.astype(BF16)], axis=1))
            p_new = (jnp.exp(glj) * p_prev.astype(F32) + jnp.exp(gsum) * n_mat.astype(F32)
                     + res[:, 0:HEAD_DIM])
            c_s[n] = p_new.astype(BF16)
            bb_s[n] = jnp.exp(glj) * q_prev + res[:, HEAD_DIM:2 * HEAD_DIM] + bb_s[n]
            gm_s[pl.ds(n, 1), :] = gsum + glj
            return carry
        lax.fori_loop(0, n_groups, compose, 0, unroll=min(n_groups, GDN_UNROLL))

    gm_s[...] = jnp.exp(gm_s[...])

    def chain(gi, e):
        eb = e.astype(BF16)
        e_s[gi] = e
        st_s[gi * grp] = eb
        nl = gi * grp + (grp - 1)
        return e * gm_s[pl.ds(nl, 1), :] + _dot(c_s[nl], eb) + bb_s[nl]
    lax.fori_loop(0, n_groups, chain, jnp.zeros((HEAD_DIM, HEAD_DIM), F32))

    def fill(gi, carry):
        e = e_s[gi]
        eb = st_s[gi * grp]
        for jj in range(1, grp):
            n = gi * grp + jj
            st_s[n] = (e * gm_s[pl.ds(n - 1, 1), :] + _dot(c_s[n - 1], eb) + bb_s[n - 1]).astype(BF16)
        return carry
    lax.fori_loop(0, n_groups, fill, 0, unroll=min(n_groups, GDN_UNROLL))

    nw = nw_ref[...]

    def emit(n, carry):
        r = rows_of(n, c)
        o = _dot(qp_s[r, :], st_s[n]) + op_s[r, :]
        o = o * lax.rsqrt(jnp.mean(o * o, axis=-1, keepdims=True) + EPS) * nw
        o_ref[r, :] = (o * _silu(z_ref[r, :].astype(F32))).astype(o_ref.dtype)
        return carry
    lax.fori_loop(0, n_chunks, emit, 0, unroll=min(n_chunks, GDN_UNROLL))


def _gdn(proj3, small3, conv_w, alane, dlane, gdn_norm_w):
    b, s, _ = proj3.shape
    c = GDN_CHUNK
    seq_spec = lambda col: pl.BlockSpec((None, s, HEAD_DIM), lambda bi, h: (bi, 0, col + h))
    cw_spec = lambda off: pl.BlockSpec((CONV_WIDTH, HEAD_DIM), lambda bi, h: (0, off + h))
    row_spec = pl.BlockSpec((1, LANES), lambda bi, h: (0, 0))
    seq = lambda dt, w=HEAD_DIM: pltpu.VMEM((s, w), dt)
    return pl.pallas_call(
        _gdn_kernel,
        grid=(b, N_HEADS),
        in_specs=[
            seq_spec(COL_QB), seq_spec(COL_KB), seq_spec(COL_VB), seq_spec(COL_ZB),
            pl.BlockSpec((None, s, LANES), lambda bi, h: (bi, 0, 0)),
            cw_spec(0), cw_spec(N_HEADS), cw_spec(2 * N_HEADS),
            row_spec, row_spec, row_spec,
        ],
        out_specs=pl.BlockSpec((None, s, HEAD_DIM), lambda bi, h: (bi, 0, h)),
        out_shape=jax.ShapeDtypeStruct((b, s, N_HEADS * HEAD_DIM), BF16),
        scratch_shapes=[
            pltpu.VMEM((s + GDN_PAD, LANES), F32),
            seq(F32), seq(F32),
            pltpu.VMEM((s // GDN_PAIR, LANES, GDN_PAIR), F32),
            seq(F32), seq(F32),
            seq(BF16), seq(F32),
            seq(BF16), seq(BF16),
            pltpu.VMEM((s // GDN_PAIR, HEAD_DIM, GDN_PAIR), BF16),
            seq(BF16, 2 * HEAD_DIM),
            seq(F32), seq(F32), seq(BF16),
            seq(BF16, 2 * HEAD_DIM),
            pltpu.VMEM((s // c, HEAD_DIM, HEAD_DIM), BF16),
            pltpu.VMEM((s // c, HEAD_DIM, HEAD_DIM), F32),
            seq(BF16), seq(F32),
            pltpu.VMEM((s // c, HEAD_DIM, HEAD_DIM), BF16),
            pltpu.VMEM((s // c, LANES), F32),
            pltpu.VMEM((s // c // GDN_SCAN_GROUP, HEAD_DIM, HEAD_DIM), F32),
        ],
        compiler_params=pltpu.CompilerParams(
            dimension_semantics=("arbitrary", "arbitrary"),
            vmem_limit_bytes=VMEM_LIMIT),
        name="gated_deltanet",
    )(proj3, proj3, proj3, proj3, small3, conv_w, conv_w, conv_w, alane, dlane, gdn_norm_w)


def _out_kernel(x_ref, a_ref, b_ref, ga_ref, gb_ref, wa_ref, wb_ref, wo_ref, nw_ref, o_ref):
    y_a = _dot(a_ref[...], wa_ref[...])
    y_b = _dot(b_ref[...], wb_ref[...])
    merged = (_sigmoid(ga_ref[...].astype(F32)) * y_a
              + _sigmoid(gb_ref[...].astype(F32)) * y_b)
    out = _dot(merged.astype(BF16), wo_ref[...])
    ms = jnp.mean(out * out, axis=-1, keepdims=True)
    o_ref[...] = x_ref[...] + out * lax.rsqrt(ms + EPS) * nw_ref[...]


def _out_projection(x2, a2, b2, proj2, wa, wb, wo, post_norm_w, tm=512):
    t = x2.shape[0]
    tok = lambda: pl.BlockSpec((tm, D_MODEL), lambda i: (i, 0))
    full = lambda: pl.BlockSpec((D_MODEL, D_MODEL), lambda i: (0, 0))
    return pl.pallas_call(
        _out_kernel,
        grid=(t // tm,),
        in_specs=[
            tok(), tok(), tok(),
            pl.BlockSpec((tm, D_MODEL), lambda i: (i, BLK_GATE_A)),
            pl.BlockSpec((tm, D_MODEL), lambda i: (i, BLK_GATE_B)),
            full(), full(), full(),
            pl.BlockSpec((1, D_MODEL), lambda i: (0, 0)),
        ],
        out_specs=tok(),
        out_shape=jax.ShapeDtypeStruct((t, D_MODEL), F32),
        compiler_params=pltpu.CompilerParams(
            dimension_semantics=("arbitrary",),
            vmem_limit_bytes=VMEM_LIMIT),
        name="out_projection",
    )(x2, a2, b2, proj2, proj2, wa, wb, wo, post_norm_w)


def _rope_tables(s):
    half = ROPE_DIM // 2
    inv_freq = np.power(ROPE_THETA, -np.arange(half, dtype=np.float64) * (2.0 / ROPE_DIM))
    ang = np.arange(s, dtype=np.float64)[:, None] * inv_freq[None, :]
    cos, sin = np.cos(ang), np.sin(ang)
    ones = np.ones((s, HEAD_DIM - ROPE_DIM))
    zeros = np.zeros((s, HEAD_DIM - ROPE_DIM))
    cos_t = np.concatenate([cos, cos, ones], axis=1).astype(np.float32)
    sin_t = np.concatenate([-sin, sin, zeros], axis=1).astype(np.float32)
    return jnp.asarray(cos_t), jnp.asarray(sin_t)


def _head_lanes(v):
    return jnp.pad(v.astype(F32), (LANE_DECAY, LANES - LANE_DECAY - N_HEADS)).reshape(1, LANES)


def _layer(x, pre_norm_w, w_in, conv_w, a_log, dt_bias, gdn_norm_w,
           w_branch_a, w_branch_b, w_out, post_norm_w):
    b, s, d = x.shape
    x2 = x.reshape(b * s, d)
    n_main = 8 * D_MODEL
    w_main = jnp.concatenate([w_in[:, :n_main], w_in[:, n_main + 2 * N_HEADS:]], axis=1).astype(BF16)
    w_small = jnp.pad(w_in[:, n_main:n_main + 2 * N_HEADS],
                      ((0, 0), (0, LANES - 2 * N_HEADS))).astype(BF16)

    proj, small = _in_projection(x2, pre_norm_w.reshape(1, d), w_main, w_small)
    proj3 = proj.reshape(b, s, N_MAIN_BLOCKS * D_MODEL)
    small3 = small.reshape(b, s, LANES)

    cos_t, sin_t = _rope_tables(s)
    act_a = _moba(proj3, cos_t, sin_t)
    act_b = _gdn(proj3, small3, conv_w, _head_lanes(a_log), _head_lanes(dt_bias),
                 gdn_norm_w.reshape(1, HEAD_DIM))

    out = _out_projection(x2, act_a.reshape(b * s, d), act_b.reshape(b * s, d), proj,
                          w_branch_a.astype(BF16), w_branch_b.astype(BF16), w_out.astype(BF16),
                          post_norm_w.reshape(1, d))
    return out.reshape(b, s, d)


def kernel(x, pre_norm_w, w_in, conv_w, a_log, dt_bias, gdn_norm_w,
           w_branch_a, w_branch_b, w_out, post_norm_w):
    for layer in range(pre_norm_w.shape[0]):
        x = _layer(x, pre_norm_w[layer], w_in[layer], conv_w[layer], a_log[layer],
                   dt_bias[layer], gdn_norm_w[layer], w_branch_a[layer],
                   w_branch_b[layer], w_out[layer], post_norm_w[layer])
    return x
```
